```python
import math
import jax, jax.numpy as jnp
from jax import lax
import numpy as np

D_MODEL = 2048
BATCH = 4
SEQ = 4096
DEPTH = 2
DEC_BATCH = 8
DEC_SEQ = 64
PAST_LEN = 4096

CHUNK = 64
N_AB = (DEPTH + 1) // 2
N_C = DEPTH // 2
D_A = D_MODEL // 2
G_A = 4
C_A = D_A // G_A
MLP_CHUNK = 128
D_B = D_MODEL // 2
HEAD_B = 64
H_B = D_B // HEAD_B
R_W = 64
R_A = 64
D_B_SHIFT = 3 * D_B + R_W + R_A
HEAD_C = 128
H_C = D_MODEL // (2 * HEAD_C)
D_C = H_C * 2 * HEAD_C
Q_BLOCK = 128
P_AB = 3 * D_A + D_B_SHIFT + D_B
P_C = 4 * D_C
DEEPNORM_ALPHA = (2 * DEPTH) ** 0.25
DEEPNORM_BETA = (8 * DEPTH) ** -0.25
LN_EPS = 1e-5
GN_EPS_B = 64e-5
SUBLN_EPS = 1e-5

kernel_name = "hybrid_streaming_gmlp_rwkv7_diffattn_step"


def layer_norm(x, g, b, eps=LN_EPS):
    xf = x.astype(jnp.float32)
    mu = jnp.mean(xf, -1, keepdims=True)
    var = jnp.mean(jnp.square(xf - mu), -1, keepdims=True)
    return ((xf - mu) * lax.rsqrt(var + eps) * g + b).astype(x.dtype)


def spatial_gate(v_n, ws, bs):
    b, t, _ = v_n.shape
    L = min(t, MLP_CHUNK)
    n = t // L
    mask = jnp.tril(jnp.ones((L, L), dtype=bool))
    w = jnp.where(mask, ws[:, :L, :L], 0)
    vg = v_n.reshape(b, n, L, G_A, C_A)
    bias = jnp.swapaxes(bs[:, :L], 0, 1)[None, None, :, :, None]
    out = jnp.einsum('gts,bnsgc->bntgc', w, vg) + bias
    return out.reshape(b, t, D_A)


def rwkv7_scan(s0, r, decay, k, v, kk, a):
    def step(s, inp):
        r_t, w_t, k_t, v_t, kk_t, a_t = inp
        s_kk = jnp.einsum('bhvk,bhk->bhv', s, kk_t)
        s = (s * w_t[:, :, None, :] - s_kk[..., None] * (kk_t * a_t)[:, :, None, :]
             + v_t[..., None] * k_t[:, :, None, :])
        return s, jnp.einsum('bhvk,bhk->bhv', s, r_t)
    xs = tuple(jnp.swapaxes(z.astype(jnp.float32), 0, 1) for z in (r, decay, k, v, kk, a))
    s_fin, y = lax.scan(step, s0.astype(jnp.float32), xs)
    return s_fin, jnp.swapaxes(y, 0, 1)


def ab_layer(x, shift_prev, wkv0, w_in, a_ln_g, a_ln_b, a_ws, a_bs, b_mu, b_w0, b_w2,
             b_a0, b_a2, b_kk, b_ka, b_rk, b_lnx_g, b_lnx_b, w_out, ln_g, ln_b):
    f32 = jnp.float32
    bsz, t, _ = x.shape
    proj = x @ w_in
    u, v_a, g_a, h_b, g_b = jnp.split(proj, [D_A, 2 * D_A, 3 * D_A, 3 * D_A + D_B_SHIFT], axis=-1)
    v_n = layer_norm(v_a, a_ln_g, a_ln_b)
    out_a = u * spatial_gate(v_n, a_ws, a_bs) * jax.nn.silu(g_a)
    prev = jnp.concatenate([shift_prev[:, None].astype(h_b.dtype), h_b[:, :-1]], axis=1)
    hs = h_b + (prev - h_b) * b_mu
    r, k, v, wl, al = jnp.split(hs, [D_B, 2 * D_B, 3 * D_B, 3 * D_B + R_W], axis=-1)
    w = -jax.nn.softplus(-(b_w0 + jnp.tanh(wl) @ b_w2).astype(f32)) - 0.5
    decay = jnp.exp(-jnp.exp(w))
    a = jax.nn.sigmoid((b_a0 + al @ b_a2).astype(f32))
    heads = lambda z: z.astype(f32).reshape(bsz, t, H_B, HEAD_B)
    kk = heads(k * b_kk)
    kk = kk / jnp.maximum(jnp.sqrt(jnp.sum(jnp.square(kk), -1, keepdims=True)), 1e-12)
    k_mod = k.astype(f32) * (1.0 + (a - 1.0) * b_ka)
    r_h, k_h, v_h, a_h, w_h = heads(r), heads(k_mod), heads(v), heads(a), heads(decay)
    wkv_new, y = rwkv7_scan(wkv0, r_h, w_h, k_h, v_h, kk, a_h)
    mu_y = jnp.mean(y, -1, keepdims=True)
    var_y = jnp.mean(jnp.square(y - mu_y), -1, keepdims=True)
    y = ((y - mu_y) * lax.rsqrt(var_y + GN_EPS_B)).reshape(bsz, t, D_B) * b_lnx_g + b_lnx_b
    bonus = (jnp.sum(r_h * k_h * b_rk, -1, keepdims=True) * v_h).reshape(bsz, t, D_B)
    out_b = ((y + bonus) * jax.nn.silu(g_b.astype(f32))).astype(x.dtype)
    out = jnp.concatenate([out_a, out_b], axis=-1) @ w_out
    x_new = layer_norm(DEEPNORM_ALPHA * x + out, ln_g, ln_b)
    return x_new, h_b[:, -1], wkv_new, v_n


def diff_attend(q, k, v, mask, lam):
    s = jnp.einsum('bqhjd,bkhjd->jbhqk', q, k).astype(jnp.float32) * (HEAD_C ** -0.5)
    if mask is not None:
        s = jnp.where(mask, s, -jnp.inf)
    p = jax.nn.softmax(s, axis=-1)
    attn = p[0] - lam * p[1]
    return jnp.einsum('bhqk,bkhe->bqhe', attn.astype(v.dtype), v)


def c_layer(x, k_cache, v_cache, lam_init, w_in, lq1, lk1, lq2, lk2, subln_g, w_out, ln_g, ln_b):
    f32 = jnp.float32
    bsz, t, _ = x.shape
    q, k, v, g = jnp.split(x @ w_in, 4, axis=-1)
    q = q.reshape(bsz, t, H_C, 2, HEAD_C)
    k = k.reshape(bsz, t, H_C, 2, HEAD_C)
    v = v.reshape(bsz, t, H_C, 2 * HEAD_C)
    lam = (jnp.exp(jnp.sum((lq1 * lk1).astype(f32))) - jnp.exp(jnp.sum((lq2 * lk2).astype(f32)))
           + lam_init)
    if k_cache is None:
        nb = t // Q_BLOCK
        qb = jnp.moveaxis(q.reshape(bsz, nb, Q_BLOCK, H_C, 2, HEAD_C), 1, 0)
        key_pos = jnp.arange(t)
        def block(args):
            q_blk, i = args
            q_pos = i * Q_BLOCK + jnp.arange(Q_BLOCK)
            limit = (q_pos // CHUNK + 1) * CHUNK
            return diff_attend(q_blk, k, v, key_pos[None, :] < limit[:, None], lam)
        o = lax.map(block, (qb, jnp.arange(nb)))
        o = jnp.moveaxis(o, 0, 1).reshape(bsz, t, H_C, 2 * HEAD_C)
    else:
        k_all = jnp.concatenate([k_cache.astype(k.dtype), k], axis=1)
        v_all = jnp.concatenate([v_cache.astype(v.dtype), v], axis=1)
        o = diff_attend(q, k_all, v_all, None, lam)
    of = o.astype(f32)
    of = of * lax.rsqrt(jnp.mean(jnp.square(of), -1, keepdims=True) + SUBLN_EPS) * subln_g
    of = of * (1.0 - lam_init)
    o = (of.reshape(bsz, t, D_C) * jax.nn.silu(g.astype(f32))).astype(x.dtype)
    out = o @ w_out
    return layer_norm(DEEPNORM_ALPHA * x + out, ln_g, ln_b), k, v


def setup_inputs(seed: int = 0) -> dict:
    key = jax.random.key(seed)
    ks = iter(jax.random.split(key, 48))
    f32 = jnp.float32
    nrm = lambda shape, scale: jax.random.normal(next(ks), shape, f32) * scale
    return {
        "x_prompt": nrm((BATCH, SEQ, D_MODEL), 1.0),
        "x_sample": nrm((DEC_BATCH, DEC_SEQ, D_MODEL), 1.0),
        "state_b_shift": nrm((N_AB, DEC_BATCH, D_B_SHIFT), 1.0),
        "state_b_wkv": nrm((N_AB, DEC_BATCH, H_B, HEAD_B, HEAD_B), 0.3),
        "cache_c_k": nrm((N_C, DEC_BATCH, PAST_LEN, H_C, 2, HEAD_C), 1.0),
        "cache_c_v": nrm((N_C, DEC_BATCH, PAST_LEN, H_C, 2 * HEAD_C), 1.0),
        "ab_w_in": nrm((N_AB, D_MODEL, P_AB), D_MODEL ** -0.5),
        "ab_a_ln_g": 1.0 + nrm((N_AB, D_A), 0.02),
        "ab_a_ln_b": nrm((N_AB, D_A), 0.02),
        "ab_a_ws": nrm((N_AB, G_A, MLP_CHUNK, MLP_CHUNK), MLP_CHUNK ** -0.5),
        "ab_a_bs": 1.0 + nrm((N_AB, G_A, MLP_CHUNK), 0.1),
        "ab_b_mu": jax.random.uniform(next(ks), (N_AB, D_B_SHIFT), f32),
        "ab_b_w0": -3.0 + nrm((N_AB, D_B), 1.0),
        "ab_b_w2": nrm((N_AB, R_W, D_B), 0.1 * R_W ** -0.5),
        "ab_b_a0": nrm((N_AB, D_B), 0.1),
        "ab_b_a2": nrm((N_AB, R_A, D_B), 0.1 * R_A ** -0.5),
        "ab_b_kk": 0.85 + nrm((N_AB, D_B), 0.02),
        "ab_b_ka": 1.0 + nrm((N_AB, D_B), 0.02),
        "ab_b_rk": nrm((N_AB, H_B, HEAD_B), 0.1),
        "ab_b_lnx_g": 1.0 + nrm((N_AB, D_B), 0.02),
        "ab_b_lnx_b": nrm((N_AB, D_B), 0.02),
        "ab_w_out": nrm((N_AB, D_A + D_B, D_MODEL), (D_A + D_B) ** -0.5 * DEEPNORM_BETA),
        "ab_ln_g": 1.0 + nrm((N_AB, D_MODEL), 0.02),
        "ab_ln_b": nrm((N_AB, D_MODEL), 0.02),
        "c_w_in": nrm((N_C, D_MODEL, P_C), D_MODEL ** -0.5),
        "c_lam_q1": nrm((N_C, HEAD_C), 0.1),
        "c_lam_k1": nrm((N_C, HEAD_C), 0.1),
        "c_lam_q2": nrm((N_C, HEAD_C), 0.1),
        "c_lam_k2": nrm((N_C, HEAD_C), 0.1),
        "c_subln_g": 1.0 + nrm((N_C, 2 * HEAD_C), 0.02),
        "c_w_out": nrm((N_C, D_C, D_MODEL), D_C ** -0.5 * DEEPNORM_BETA),
        "c_ln_g": 1.0 + nrm((N_C, D_MODEL), 0.02),
        "c_ln_b": nrm((N_C, D_MODEL), 0.02),
    }


def reference(x_prompt, x_sample, state_b_shift, state_b_wkv, cache_c_k, cache_c_v,
              ab_w_in, ab_a_ln_g, ab_a_ln_b, ab_a_ws, ab_a_bs, ab_b_mu, ab_b_w0, ab_b_w2,
              ab_b_a0, ab_b_a2, ab_b_kk, ab_b_ka, ab_b_rk, ab_b_lnx_g, ab_b_lnx_b, ab_w_out,
              ab_ln_g, ab_ln_b, c_w_in, c_lam_q1, c_lam_k1, c_lam_q2, c_lam_k2, c_subln_g,
              c_w_out, c_ln_g, c_ln_b):
    x_p, x_s = x_prompt, x_sample
    sh_p_l, wkv_p_l, sh_s_l, wkv_s_l, va_s_l = [], [], [], [], []
    kp_l, vp_l, ks_l, vs_l = [], [], [], []
    for li in range(DEPTH):
        j = li // 2
        if li % 2 == 0:
            prm = (ab_w_in[j], ab_a_ln_g[j], ab_a_ln_b[j], ab_a_ws[j], ab_a_bs[j], ab_b_mu[j],
                   ab_b_w0[j], ab_b_w2[j], ab_b_a0[j], ab_b_a2[j], ab_b_kk[j], ab_b_ka[j],
                   ab_b_rk[j], ab_b_lnx_g[j], ab_b_lnx_b[j], ab_w_out[j], ab_ln_g[j], ab_ln_b[j])
            shift0 = jnp.zeros((x_p.shape[0], D_B_SHIFT), x_p.dtype)
            wkv0 = jnp.zeros((x_p.shape[0], H_B, HEAD_B, HEAD_B), jnp.float32)
            x_p, sh_p, wkv_p, _ = ab_layer(x_p, shift0, wkv0, *prm)
            x_s, sh_s, wkv_s, va_s = ab_layer(x_s, state_b_shift[j], state_b_wkv[j], *prm)
            sh_p_l.append(sh_p.astype(state_b_shift.dtype))
            wkv_p_l.append(wkv_p.astype(state_b_wkv.dtype))
            sh_s_l.append(sh_s.astype(state_b_shift.dtype))
            wkv_s_l.append(wkv_s.astype(state_b_wkv.dtype))
            va_s_l.append(va_s)
        else:
            lam_init = 0.8 - 0.6 * math.exp(-0.3 * li)
            prm = (c_w_in[j], c_lam_q1[j], c_lam_k1[j], c_lam_q2[j], c_lam_k2[j], c_subln_g[j],
                   c_w_out[j], c_ln_g[j], c_ln_b[j])
            x_p, k_p, v_p = c_layer(x_p, None, None, lam_init, *prm)
            x_s, k_s, v_s = c_layer(x_s, cache_c_k[j], cache_c_v[j], lam_init, *prm)
            kp_l.append(k_p.astype(cache_c_k.dtype))
            vp_l.append(v_p.astype(cache_c_v.dtype))
            ks_l.append(k_s.astype(cache_c_k.dtype))
            vs_l.append(v_s.astype(cache_c_v.dtype))
    return (x_p, x_s, jnp.stack(sh_p_l), jnp.stack(wkv_p_l), jnp.stack(sh_s_l), jnp.stack(wkv_s_l),
            jnp.stack(va_s_l), jnp.stack(kp_l), jnp.stack(vp_l), jnp.stack(ks_l), jnp.stack(vs_l))
```

```python
import functools
import math

import jax
import jax.numpy as jnp
from jax import lax
from jax.experimental import pallas as pl
from jax.experimental.pallas import tpu as pltpu

F32 = jnp.float32
BF16 = jnp.bfloat16

D_MODEL = 2048
DEPTH = 2
D_A = 1024
G_A = 4
C_A = D_A // G_A
MLP_CHUNK = 128
D_B = 1024
HEAD_B = 64
H_B = D_B // HEAD_B
R_W = 64
R_A = 64
D_B_SHIFT = 3 * D_B + R_W + R_A
HEAD_C = 128
H_C = 8
D_C = 2048
ATT_CHUNK = 64
DEEPNORM_ALPHA = (2 * DEPTH) ** 0.25
LN_EPS = 1e-5
GN_EPS_B = 64e-5
SUBLN_EPS = 1e-5

QUAD = 4 * HEAD_B
N_QUAD = D_B // QUAD
SCAN_L = 64
VMEM_LIMIT = 52 * 1024 * 1024


def _cparams(sem):
    return pltpu.CompilerParams(dimension_semantics=sem, vmem_limit_bytes=VMEM_LIMIT)


def _dot(a, b, prec=None):
    return jnp.dot(a, b, precision=prec, preferred_element_type=F32)


def _dot_nt(a, b, prec=None):
    return lax.dot_general(a, b, (((1,), (1,)), ((), ())), precision=prec, preferred_element_type=F32)


def _dot_tn(a, b, prec=None):
    return lax.dot_general(a, b, (((0,), (0,)), ((), ())), precision=prec, preferred_element_type=F32)


def _silu(x):
    return x * jax.nn.sigmoid(x)


def _mm_kernel(x_ref, w_ref, o_ref, xb_ref):
    @pl.when(pl.program_id(1) == 0)
    def _():
        xb_ref[...] = x_ref[...].astype(BF16)

    o_ref[...] = _dot(xb_ref[...], w_ref[...]).astype(o_ref.dtype)


def _matmul(x, w, tm, tn):
    m, k = x.shape
    n = w.shape[1]
    return pl.pallas_call(
        _mm_kernel,
        grid=(m // tm, n // tn),
        in_specs=[pl.BlockSpec((tm, k), lambda i, j: (i, 0)),
                  pl.BlockSpec((k, tn), lambda i, j: (0, j))],
        out_specs=pl.BlockSpec((tm, tn), lambda i, j: (i, j)),
        out_shape=jax.ShapeDtypeStruct((m, n), F32),
        scratch_shapes=[pltpu.VMEM((tm, k), BF16)],
        name="proj_matmul",
        compiler_params=_cparams(("parallel", "arbitrary")),
    )(x, w)


def _mixa_kernel(u_ref, v_ref, g_ref, lng_ref, lnb_ref, wm_ref, bias_ref, oa_ref, vn_ref, *, chunk, n_chunks):
    v = v_ref[...]
    mu = jnp.mean(v, -1, keepdims=True)
    vc = v - mu
    var = jnp.mean(vc * vc, -1, keepdims=True)
    vn = vc * lax.rsqrt(var + LN_EPS) * lng_ref[...] + lnb_ref[...]
    vn_ref[...] = vn
    vnb = vn.astype(BF16)
    for c in range(n_chunks):
        rows = slice(c * chunk, (c + 1) * chunk)
        parts = [_dot(wm_ref[g], vnb[rows, g * C_A:(g + 1) * C_A]) for g in range(G_A)]
        sg = jnp.concatenate(parts, axis=-1) + bias_ref[...]
        oa_ref[rows, :] = u_ref[rows, :] * sg * _silu(g_ref[rows, :])


def _mix_a(proj, lng, lnb, wm, bias, chunk, n_chunks):
    m = proj.shape[0]
    tm = chunk * n_chunks
    blk = lambda j: pl.BlockSpec((tm, D_A), lambda i, j=j: (i, j))
    full = lambda a: pl.BlockSpec(a.shape, lambda i: (0,) * a.ndim)
    return pl.pallas_call(
        functools.partial(_mixa_kernel, chunk=chunk, n_chunks=n_chunks),
        grid=(m // tm,),
        in_specs=[blk(0), blk(1), blk(2), full(lng), full(lnb), full(wm), full(bias)],
        out_specs=[pl.BlockSpec((tm, D_A), lambda i: (i, 0))] * 2,
        out_shape=[jax.ShapeDtypeStruct((m, D_A), F32)] * 2,
        name="mixer_a",
        compiler_params=_cparams(("parallel",)),
    )(proj, proj, proj, lng, lnb, wm, bias)


def _rwkv_kernel(r_ref, k_ref, v_ref, g_ref, wa_ref, sh0_ref, s0_ref, mu_ref, w0_ref, w2_ref, a0_ref,
                 a2_ref, kkw_ref, kaw_ref, rkw_ref, lng_ref, lnb_ref, ob_ref, sout_ref, prev_scr, st_scr,
                 *, prec):
    L = SCAN_L
    c = pl.program_id(1)

    @pl.when(c == 0)
    def _():
        prev_scr[...] = sh0_ref[0]
        st_scr[...] = s0_ref[0]

    def shifted(ref, idx, width):
        x = ref[...]
        first = lax.broadcasted_iota(jnp.int32, (L, width), 0) == 0
        prev = jnp.where(first, prev_scr[idx:idx + 1, :width], pltpu.roll(x, 1, 0))
        prev_scr[idx:idx + 1, :width] = x[L - 1:L, :]
        return x + (prev - x) * mu_ref[idx:idx + 1, :width]

    r = shifted(r_ref, 0, D_B)
    k = shifted(k_ref, 1, D_B)
    v = shifted(v_ref, 2, D_B)
    wa = shifted(wa_ref, 3, R_W + R_A)

    z = w0_ref[...] + _dot(jnp.tanh(wa).astype(BF16), w2_ref[...])
    nz = -z
    softplus = jnp.maximum(nz, 0.0) + jnp.log(1.0 + jnp.exp(-jnp.abs(nz)))
    log_decay = -jnp.exp(-softplus - 0.5)
    a = jax.nn.sigmoid(a0_ref[...] + _dot(wa.astype(BF16), a2_ref[...]))

    row4 = lax.broadcasted_iota(jnp.int32, (QUAD, QUAD), 0)
    col4 = lax.broadcasted_iota(jnp.int32, (QUAD, QUAD), 1)
    same_head = (row4 // HEAD_B) == (col4 // HEAD_B)
    ones_head = same_head.astype(F32)

    def head_sum(x):
        return jnp.concatenate(
            [_dot(x[:, q * QUAD:(q + 1) * QUAD], ones_head, lax.Precision.HIGHEST) for q in range(N_QUAD)], axis=-1)

    kk = k * kkw_ref[...]
    kk = kk / jnp.maximum(jnp.sqrt(head_sum(kk * kk)), 1e-12)
    k_mod = k * (1.0 + (a - 1.0) * kaw_ref[...])
    bonus = head_sum(r * k_mod * rkw_ref[...]) * v

    tl = lax.broadcasted_iota(jnp.int32, (L, L), 0)
    sl = lax.broadcasted_iota(jnp.int32, (L, L), 1)
    cum = _dot((sl <= tl).astype(F32), log_decay, lax.Precision.HIGHEST)
    cum_last = cum[L - 1:L, :]
    p_in = jnp.exp(cum)
    p_inv = jnp.exp(-cum)
    p_tail = jnp.exp(cum_last - cum)
    kap = kk * jnp.exp(cum - log_decay)
    beta = kk * a
    bet = beta * p_inv
    kt = k_mod * p_inv
    rt = r * p_in
    kt_tail = k_mod * p_tail
    bet_tail = beta * p_tail
    p_last = jnp.exp(cum_last)

    tq = lax.broadcasted_iota(jnp.int32, (L, QUAD), 0)
    sq = lax.broadcasted_iota(jnp.int32, (L, QUAD), 1) % HEAD_B
    strict = sq < tq
    incl = sq <= tq
    eye_c = (sq == tq).astype(F32)

    def same_blk(b):
        return (tq // b) == (sq // b)

    def bd(x):
        return jnp.where(same_head, jnp.concatenate([x] * (QUAD // L), axis=0), 0.0)

    ys = []
    for q in range(N_QUAD):
        lanes = slice(q * QUAD, (q + 1) * QUAD)
        st = st_scr[q]
        kr = jnp.concatenate([kap[:, lanes], rt[:, lanes]], axis=0)
        g_b = _dot_nt(kr, bd(bet[:, lanes]), prec)
        g_k = _dot_nt(kr, bd(kt[:, lanes]), prec)
        g_s = _dot_nt(kr, st, prec)
        n_kb = jnp.where(strict, g_b[:L], 0.0)
        a_kk = jnp.where(strict, g_k[:L], 0.0)
        q_b = jnp.where(incl, g_b[L:], 0.0)
        q_k = jnp.where(incl, g_k[L:], 0.0)

        m1 = -jnp.where(same_blk(16), n_kb, 0.0)
        t_inv = eye_c + m1
        pw = m1
        for _ in range(3):
            pw = _dot(pw, bd(pw), prec)
            t_inv = t_inv + _dot(t_inv, bd(pw), prec)
        for inner, outer in ((16, 32), (32, 64)):
            off = jnp.where(same_blk(outer) & jnp.logical_not(same_blk(inner)), n_kb, 0.0)
            t_inv = t_inv - _dot(t_inv, bd(_dot(off, bd(t_inv), prec)), prec)

        v_q = v[:, lanes]
        bd_v = bd(v_q)
        u_q = _dot(t_inv, bd(g_s[:L] + _dot(a_kk, bd_v, prec)), prec)
        ys.append(g_s[L:] + _dot(q_k, bd_v, prec) - _dot(q_b, bd(u_q), prec))
        upd = _dot_tn(jnp.concatenate([v_q, u_q], axis=0),
                      jnp.concatenate([kt_tail[:, lanes], -bet_tail[:, lanes]], axis=0), prec)
        st_scr[q] = jnp.where(same_head, st * p_last[:, lanes] + upd, 0.0)

    y = jnp.concatenate(ys, axis=-1)
    mu_y = head_sum(y) * (1.0 / HEAD_B)
    yc = y - mu_y
    var_y = head_sum(yc * yc) * (1.0 / HEAD_B)
    yn = yc * lax.rsqrt(var_y + GN_EPS_B) * lng_ref[...] + lnb_ref[...]
    ob_ref[...] = (yn + bonus) * _silu(g_ref[...])
    sout_ref[0] = st_scr[...]


def _rwkv(proj, wa, sh0, s0, prm, bsz, t, prec):
    nc = t // SCAN_L
    col = lambda j: pl.BlockSpec((SCAN_L, D_B), lambda b, c, j=j: (b * nc + c, j))
    full = lambda a: pl.BlockSpec(a.shape, lambda b, c: (0,) * a.ndim)
    per_b = lambda a: pl.BlockSpec((1,) + a.shape[1:], lambda b, c: (b,) + (0,) * (a.ndim - 1))
    return pl.pallas_call(
        functools.partial(_rwkv_kernel, prec=prec),
        grid=(bsz, nc),
        in_specs=[col(3), col(4), col(5), col(6),
                  pl.BlockSpec((SCAN_L, R_W + R_A), lambda b, c: (b * nc + c, 0)),
                  per_b(sh0), per_b(s0)] + [full(p) for p in prm],
        out_specs=[pl.BlockSpec((SCAN_L, D_B), lambda b, c: (b * nc + c, 0)), per_b(s0)],
        out_shape=[jax.ShapeDtypeStruct((bsz * t, D_B), F32), jax.ShapeDtypeStruct(s0.shape, F32)],
        scratch_shapes=[pltpu.VMEM((4, D_B), F32), pltpu.VMEM((N_QUAD, QUAD, QUAD), F32)],
        name="rwkv_scan",
        compiler_params=_cparams(("parallel", "arbitrary")),
    )(proj, proj, proj, proj, wa, sh0, s0, *prm)


def _outproj_ln_kernel(a_ref, b_ref, x_ref, w1_ref, w2_ref, g_ref, bias_ref, o_ref):
    acc = _dot(a_ref[...].astype(BF16), w1_ref[...]) + _dot(b_ref[...].astype(BF16), w2_ref[...])
    y = DEEPNORM_ALPHA * x_ref[...] + acc
    mu = jnp.mean(y, -1, keepdims=True)
    yc = y - mu
    var = jnp.mean(yc * yc, -1, keepdims=True)
    o_ref[...] = yc * lax.rsqrt(var + LN_EPS) * g_ref[...] + bias_ref[...]


def _outproj_ln(a, a_col, b, b_col, x, w1, w2, g, bias, tm):
    m = x.shape[0]
    half = D_MODEL // 2
    full = lambda arr: pl.BlockSpec(arr.shape, lambda i: (0,) * arr.ndim)
    return pl.pallas_call(
        _outproj_ln_kernel,
        grid=(m // tm,),
        in_specs=[pl.BlockSpec((tm, half), lambda i: (i, a_col)),
                  pl.BlockSpec((tm, half), lambda i: (i, b_col)),
                  pl.BlockSpec((tm, D_MODEL), lambda i: (i, 0)),
                  full(w1), full(w2), full(g), full(bias)],
        out_specs=pl.BlockSpec((tm, D_MODEL), lambda i: (i, 0)),
        out_shape=jax.ShapeDtypeStruct((m, D_MODEL), F32),
        name="outproj_ln",
        compiler_params=_cparams(("parallel",)),
    )(a, b, x, w1, w2, g, bias)


QK_SCALE = (HEAD_C ** -0.5) * math.log2(math.e)


def _lambda(lq1_ref, lk1_ref, lq2_ref, lk2_ref, lam_init):
    s1 = jnp.sum(lq1_ref[...] * lk1_ref[...], -1, keepdims=True)
    s2 = jnp.sum(lq2_ref[...] * lk2_ref[...], -1, keepdims=True)
    return jnp.exp(s1) - jnp.exp(s2) + lam_init


def _attn_finish(o, g, subg_ref, lam_init):
    o = o * lax.rsqrt(jnp.mean(o * o, -1, keepdims=True) + SUBLN_EPS) * subg_ref[...]
    return o * (1.0 - lam_init) * _silu(g)


def _attn_prompt_kernel(q_ref, k_ref, v_ref, g_ref, lq1_ref, lk1_ref, lq2_ref, lk2_ref, subg_ref, o_ref,
                        kb_ref, vb_ref, s0_ref, s1_ref, *, tq, lam_init):
    qi = pl.program_id(2)

    @pl.when(qi == 0)
    def _():
        kb_ref[...] = k_ref[...].astype(BF16)
        vb_ref[...] = v_ref[...].astype(BF16)

    lam = _lambda(lq1_ref, lk1_ref, lq2_ref, lk2_ref, lam_init)
    q = (q_ref[...] * QK_SCALE).astype(BF16)
    q0 = q[:, :HEAD_C]
    q1 = q[:, HEAD_C:]

    def scores(off):
        kblk = kb_ref[pl.ds(off, tq), :]
        return _dot_nt(q0, kblk[:, :HEAD_C]), _dot_nt(q1, kblk[:, HEAD_C:])

    def pass1(j, carry):
        m0, m1 = carry
        off = pl.multiple_of(j * tq, tq)
        s0, s1 = scores(off)
        s0_ref[:, pl.ds(off, tq)] = s0
        s1_ref[:, pl.ds(off, tq)] = s1
        return jnp.maximum(m0, jnp.max(s0, -1, keepdims=True)), jnp.maximum(m1, jnp.max(s1, -1, keepdims=True))

    neg = jnp.full((tq, 1), -jnp.inf, F32)
    m0, m1 = lax.fori_loop(0, qi, pass1, (neg, neg))
    off_d = pl.multiple_of(qi * tq, tq)
    s0, s1 = scores(off_d)
    rq = lax.broadcasted_iota(jnp.int32, (tq, tq), 0) // ATT_CHUNK
    ck = lax.broadcasted_iota(jnp.int32, (tq, tq), 1) // ATT_CHUNK
    s0 = jnp.where(ck <= rq, s0, -jnp.inf)
    s1 = jnp.where(ck <= rq, s1, -jnp.inf)
    s0_ref[:, pl.ds(off_d, tq)] = s0
    s1_ref[:, pl.ds(off_d, tq)] = s1
    m0 = jnp.maximum(m0, jnp.max(s0, -1, keepdims=True))
    m1 = jnp.maximum(m1, jnp.max(s1, -1, keepdims=True))

    def pass2(j, carry):
        l0, l1, o0, o1 = carry
        off = pl.multiple_of(j * tq, tq)
        vblk = vb_ref[pl.ds(off, tq), :]
        p0 = jnp.exp2(s0_ref[:, pl.ds(off, tq)] - m0)
        p1 = jnp.exp2(s1_ref[:, pl.ds(off, tq)] - m1)
        return (l0 + jnp.sum(p0, -1, keepdims=True), l1 + jnp.sum(p1, -1, keepdims=True),
                o0 + _dot(p0.astype(BF16), vblk), o1 + _dot(p1.astype(BF16), vblk))

    zl = jnp.zeros((tq, 1), F32)
    zo = jnp.zeros((tq, 2 * HEAD_C), F32)
    l0, l1, o0, o1 = lax.fori_loop(0, qi + 1, pass2, (zl, zl, zo, zo))
    o = o0 / l0 - lam * (o1 / l1)
    o_ref[...] = _attn_finish(o, g_ref[...], subg_ref, lam_init)


def _attn_prompt(qkvg, lam_prm, subg, bsz, t, lam_init, tq):
    nq = t // tq
    hw = 2 * HEAD_C
    full = lambda a: pl.BlockSpec(a.shape, lambda b, h, i: (0,) * a.ndim)
    return pl.pallas_call(
        functools.partial(_attn_prompt_kernel, tq=tq, lam_init=lam_init),
        grid=(bsz, H_C, nq),
        in_specs=[pl.BlockSpec((tq, hw), lambda b, h, i: (b * nq + i, h)),
                  pl.BlockSpec((t, hw), lambda b, h, i: (b, H_C + h)),
                  pl.BlockSpec((t, hw), lambda b, h, i: (b, 2 * H_C + h)),
                  pl.BlockSpec((tq, hw), lambda b, h, i: (b * nq + i, 3 * H_C + h))]
                 + [full(p) for p in lam_prm] + [full(subg)],
        out_specs=pl.BlockSpec((tq, hw), lambda b, h, i: (b * nq + i, h)),
        out_shape=jax.ShapeDtypeStruct((bsz * t, D_C), F32),
        scratch_shapes=[pltpu.VMEM((t, hw), BF16), pltpu.VMEM((t, hw), BF16),
                        pltpu.VMEM((tq, t), F32), pltpu.VMEM((tq, t), F32)],
        name="attn_prompt",
        compiler_params=_cparams(("parallel", "parallel", "arbitrary")),
    )(qkvg, qkvg, qkvg, qkvg, *lam_prm, subg)


def _attn_sample_kernel(q_ref, kn_ref, vn_ref, g_ref, kc_ref, vc_ref, lq1_ref, lk1_ref, lq2_ref, lk2_ref,
                        subg_ref, o_ref, *, lam_init):
    lam = _lambda(lq1_ref, lk1_ref, lq2_ref, lk2_ref, lam_init)
    q = (q_ref[...] * QK_SCALE).astype(BF16)
    kc = kc_ref[...].astype(BF16)
    kn = kn_ref[...].astype(BF16)
    probs = []
    for j in range(2):
        cols = slice(j * HEAD_C, (j + 1) * HEAD_C)
        sc = _dot_nt(q[:, cols], kc[:, cols])
        sn = _dot_nt(q[:, cols], kn[:, cols])
        m = jnp.maximum(jnp.max(sc, -1, keepdims=True), jnp.max(sn, -1, keepdims=True))
        pc = jnp.exp2(sc - m)
        pn = jnp.exp2(sn - m)
        inv = 1.0 / (jnp.sum(pc, -1, keepdims=True) + jnp.sum(pn, -1, keepdims=True))
        probs.append((pc * inv, pn * inv))
    ac = probs[0][0] - lam * probs[1][0]
    an = probs[0][1] - lam * probs[1][1]
    o = _dot(ac.astype(BF16), vc_ref[...].astype(BF16)) + _dot(an.astype(BF16), vn_ref[...].astype(BF16))
    o_ref[...] = _attn_finish(o, g_ref[...], subg_ref, lam_init)


def _attn_sample(qkvg, kc, vc, lam_prm, subg, bsz, t, past, lam_init):
    hw = 2 * HEAD_C
    full = lambda a: pl.BlockSpec(a.shape, lambda b, h: (0,) * a.ndim)
    new = lambda j: pl.BlockSpec((t, hw), lambda b, h, j=j: (b, j * H_C + h))
    cache = pl.BlockSpec((past, hw), lambda b, h: (b, h))
    return pl.pallas_call(
        functools.partial(_attn_sample_kernel, lam_init=lam_init),
        grid=(bsz, H_C),
        in_specs=[new(0), new(1), new(2), new(3), cache, cache] + [full(p) for p in lam_prm] + [full(subg)],
        out_specs=pl.BlockSpec((t, hw), lambda b, h: (b, h)),
        out_shape=jax.ShapeDtypeStruct((bsz * t, D_C), F32),
        name="attn_sample",
        compiler_params=_cparams(("parallel", "parallel")),
    )(qkvg, qkvg, qkvg, qkvg, kc, vc, *lam_prm, subg)


def _row(p):
    return p.reshape(1, -1).astype(F32)


def _pad_rows(x, width):
    lead = x.shape[:-1]
    main = x[..., :3 * width].reshape(lead + (3, width))
    rest = jnp.pad(x[..., 3 * width:], [(0, 0)] * len(lead) + [(0, 4 * width - x.shape[-1])])
    return jnp.concatenate([main, rest[..., None, :]], axis=-2)


def _ab_params(w_in, a_ln_g, a_ln_b, a_ws, a_bs, b_mu, b_w0, b_w2, b_a0, b_a2, b_kk, b_ka, b_rk, b_lnx_g,
               b_lnx_b, w_out, ln_g, ln_b):
    split = 3 * D_A + 3 * D_B
    w_main = jnp.concatenate([w_in[:, :split], w_in[:, split + R_W + R_A:]], axis=1).astype(BF16)
    w_tail = w_in[:, split:split + R_W + R_A].astype(BF16)
    zeros = jnp.zeros((R_W, D_B), F32)
    rwkv = (_pad_rows(b_mu, D_B), _row(b_w0),
            jnp.concatenate([b_w2, zeros], 0).astype(BF16), _row(b_a0),
            jnp.concatenate([zeros, b_a2], 0).astype(BF16),
            _row(b_kk), _row(b_ka), _row(b_rk), _row(b_lnx_g), _row(b_lnx_b))
    return dict(w_main=w_main, w_tail=w_tail, a_ln_g=_row(a_ln_g), a_ln_b=_row(a_ln_b), a_ws=a_ws, a_bs=a_bs,
                rwkv=rwkv, w_out=w_out.astype(BF16), ln_g=_row(ln_g), ln_b=_row(ln_b))


def _state_to_quads(wkv):
    b = wkv.shape[0]
    x = wkv.reshape(b, N_QUAD, 4, HEAD_B, HEAD_B).astype(F32)
    eye = jnp.eye(4, dtype=F32)
    return (x[:, :, :, :, None, :] * eye[None, None, :, None, :, None]).reshape(b, N_QUAD, QUAD, QUAD)


def _quads_to_state(st):
    b = st.shape[0]
    x = st.reshape(b, N_QUAD, 4, HEAD_B, 4, HEAD_B)
    x = jnp.diagonal(x, axis1=2, axis2=4)
    return jnp.moveaxis(x, -1, 2).reshape(b, H_B, HEAD_B, HEAD_B)


def _ab_layer(x, shift_prev, wkv0, prm, scan_prec):
    bsz, t, _ = x.shape
    m = bsz * t
    x2 = x.reshape(m, D_MODEL)
    tm = min(m, 1024)
    proj = _matmul(x2, prm["w_main"], tm, D_A)
    wa = _matmul(x2, prm["w_tail"], tm, R_W + R_A)

    chunk = min(t, MLP_CHUNK)
    n_chunks = 2 if t >= 2 * MLP_CHUNK else 1
    mask = jnp.tril(jnp.ones((chunk, chunk), dtype=bool))
    wm = jnp.where(mask, prm["a_ws"][:, :chunk, :chunk], 0).astype(BF16)
    bias = jnp.repeat(jnp.swapaxes(prm["a_bs"][:, :chunk], 0, 1), C_A, axis=1).astype(F32)
    out_a, v_n = _mix_a(proj, prm["a_ln_g"], prm["a_ln_b"], wm, bias, chunk, n_chunks)

    sh0 = _pad_rows(shift_prev.astype(F32), D_B)
    out_b, st_new = _rwkv(proj, wa, sh0, _state_to_quads(wkv0), prm["rwkv"], bsz, t, scan_prec)

    w_out = prm["w_out"]
    x_new = _outproj_ln(out_a, 0, out_b, 0, x2, w_out[:D_A], w_out[D_A:], prm["ln_g"], prm["ln_b"], min(m, 256))
    last = proj.reshape(bsz, t, -1)[:, -1]
    shift_new = jnp.concatenate([last[:, 3 * D_A:3 * D_A + 3 * D_B], wa.reshape(bsz, t, -1)[:, -1]], axis=-1)
    return x_new.reshape(bsz, t, D_MODEL), shift_new, _quads_to_state(st_new), v_n.reshape(bsz, t, D_A)


def _c_layer(x, k_cache, v_cache, lam_init, w_in, lam_prm, subg, w_out, ln_g, ln_b):
    bsz, t, _ = x.shape
    m = bsz * t
    x2 = x.reshape(m, D_MODEL)
    qkvg = _matmul(x2, w_in, min(m, 1024), 1024)
    if k_cache is None:
        o = _attn_prompt(qkvg, lam_prm, subg, bsz, t, lam_init, 256)
    else:
        past = k_cache.shape[1]
        o = _attn_sample(qkvg, k_cache.reshape(bsz * past, D_C), v_cache.reshape(bsz * past, D_C), lam_prm, subg,
                         bsz, t, past, lam_init)
    x_new = _outproj_ln(o, 0, o, 1, x2, w_out[:D_C // 2], w_out[D_C // 2:], ln_g, ln_b, min(m, 256))
    k_new = qkvg[:, D_C:2 * D_C].reshape(bsz, t, H_C, 2, HEAD_C)
    v_new = qkvg[:, 2 * D_C:3 * D_C].reshape(bsz, t, H_C, 2 * HEAD_C)
    return x_new.reshape(bsz, t, D_MODEL), k_new, v_new


def kernel(x_prompt, x_sample, state_b_shift, state_b_wkv, cache_c_k, cache_c_v, ab_w_in, ab_a_ln_g, ab_a_ln_b, ab_a_ws, ab_a_bs, ab_b_mu, ab_b_w0, ab_b_w2, ab_b_a0, ab_b_a2, ab_b_kk, ab_b_ka, ab_b_rk, ab_b_lnx_g, ab_b_lnx_b, ab_w_out, ab_ln_g, ab_ln_b, c_w_in, c_lam_q1, c_lam_k1, c_lam_q2, c_lam_k2, c_subln_g, c_w_out, c_ln_g, c_ln_b):
    scan_prec = lax.Precision.HIGHEST
    x_p, x_s = x_prompt, x_sample
    sh_p_l, wkv_p_l, sh_s_l, wkv_s_l, va_s_l = [], [], [], [], []
    kp_l, vp_l, ks_l, vs_l = [], [], [], []
    for li in range(DEPTH):
        j = li // 2
        if li % 2 == 0:
            prm = _ab_params(ab_w_in[j], ab_a_ln_g[j], ab_a_ln_b[j], ab_a_ws[j], ab_a_bs[j], ab_b_mu[j],
                             ab_b_w0[j], ab_b_w2[j], ab_b_a0[j], ab_b_a2[j], ab_b_kk[j], ab_b_ka[j],
                             ab_b_rk[j], ab_b_lnx_g[j], ab_b_lnx_b[j], ab_w_out[j], ab_ln_g[j], ab_ln_b[j])
            shift0 = jnp.zeros((x_p.shape[0], D_B_SHIFT), F32)
            wkv0 = jnp.zeros((x_p.shape[0], H_B, HEAD_B, HEAD_B), F32)
            x_p, sh_p, wkv_p, _ = _ab_layer(x_p, shift0, wkv0, prm, scan_prec)
            x_s, sh_s, wkv_s, va_s = _ab_layer(x_s, state_b_shift[j], state_b_wkv[j], prm, scan_prec)
            sh_p_l.append(sh_p)
            wkv_p_l.append(wkv_p)
            sh_s_l.append(sh_s)
            wkv_s_l.append(wkv_s)
            va_s_l.append(va_s)
        else:
            lam_init = 0.8 - 0.6 * math.exp(-0.3 * li)
            w_in = c_w_in[j].astype(BF16)
            w_out = c_w_out[j].astype(BF16)
            lam_prm = (_row(c_lam_q1[j]), _row(c_lam_k1[j]), _row(c_lam_q2[j]), _row(c_lam_k2[j]))
            args = (lam_init, w_in, lam_prm, _row(c_subln_g[j]), w_out, _row(c_ln_g[j]), _row(c_ln_b[j]))
            x_p, k_p, v_p = _c_layer(x_p, None, None, *args)
            x_s, k_s, v_s = _c_layer(x_s, cache_c_k[j], cache_c_v[j], *args)
            kp_l.append(k_p)
            vp_l.append(v_p)
            ks_l.append(k_s)
            vs_l.append(v_s)
    return (x_p, x_s, jnp.stack(sh_p_l), jnp.stack(wkv_p_l), jnp.stack(sh_s_l), jnp.stack(wkv_s_l),
            jnp.stack(va_s_l), jnp.stack(kp_l), jnp.stack(vp_l), jnp.stack(ks_l), jnp.stack(vs_l))
```

```python
import functools
import math

import jax
import jax.numpy as jnp
from jax import lax
from jax.experimental import pallas as pl
from jax.experimental.pallas import tpu as pltpu

F32 = jnp.float32
BF16 = jnp.bfloat16

D_MODEL = 2048
DEPTH = 2
D_A = 1024
G_A = 4
C_A = D_A // G_A
MLP_CHUNK = 128
D_B = 1024
HEAD_B = 64
H_B = D_B // HEAD_B
R_W = 64
R_A = 64
D_B_SHIFT = 3 * D_B + R_W + R_A
HEAD_C = 128
H_C = 8
D_C = 2048
ATT_CHUNK = 64
DEEPNORM_ALPHA = (2 * DEPTH) ** 0.25
LN_EPS = 1e-5
GN_EPS_B = 64e-5
SUBLN_EPS = 1e-5

QUAD = 4 * HEAD_B
N_QUAD = D_B // QUAD
SCAN_L = 64
VMEM_LIMIT = 52 * 1024 * 1024


def _cparams(sem):
    return pltpu.CompilerParams(dimension_semantics=sem, vmem_limit_bytes=VMEM_LIMIT)


def _dot(a, b, prec=None):
    return jnp.dot(a, b, precision=prec, preferred_element_type=F32)


def _dot_nt(a, b, prec=None):
    return lax.dot_general(a, b, (((1,), (1,)), ((), ())), precision=prec, preferred_element_type=F32)


def _dot_tn(a, b, prec=None):
    return lax.dot_general(a, b, (((0,), (0,)), ((), ())), precision=prec, preferred_element_type=F32)


def _silu(x):
    return x * jax.nn.sigmoid(x)


def _mm_kernel(x_ref, w_ref, o_ref, xb_ref):
    @pl.when(pl.program_id(1) == 0)
    def _():
        xb_ref[...] = x_ref[...].astype(BF16)

    o_ref[...] = _dot(xb_ref[...], w_ref[...]).astype(o_ref.dtype)


def _matmul(x, w, tm, tn):
    m, k = x.shape
    n = w.shape[1]
    return pl.pallas_call(
        _mm_kernel,
        grid=(m // tm, n // tn),
        in_specs=[pl.BlockSpec((tm, k), lambda i, j: (i, 0)),
                  pl.BlockSpec((k, tn), lambda i, j: (0, j))],
        out_specs=pl.BlockSpec((tm, tn), lambda i, j: (i, j)),
        out_shape=jax.ShapeDtypeStruct((m, n), F32),
        scratch_shapes=[pltpu.VMEM((tm, k), BF16)],
        name="proj_matmul",
        compiler_params=_cparams(("parallel", "arbitrary")),
    )(x, w)


QK_SCALE = (HEAD_C ** -0.5) * math.log2(math.e)
K_ROWS = 2 * H_C


def _cproj_kernel(x_ref, w_ref, *o_refs, mode):
    acc = _dot(x_ref[...], w_ref[...])
    if mode == "q":
        o_refs[0][...] = (acc * QK_SCALE).astype(BF16)
    elif mode == "k":
        leaf_ref, bf_ref = o_refs
        bf_ref[...] = acc.astype(BF16)
        for c in range(K_ROWS):
            leaf_ref[:, c, :] = acc[:, c * HEAD_C:(c + 1) * HEAD_C]
    elif mode == "v":
        leaf_ref, bf_ref = o_refs
        leaf_ref[...] = acc
        bf_ref[...] = acc.astype(BF16)
    else:
        o_refs[0][...] = acc


def _cproj(xb, w, mode, tm):
    m, k = xb.shape
    n = w.shape[1]
    flat = lambda dt: (pl.BlockSpec((tm, n), lambda i: (i, 0)), jax.ShapeDtypeStruct((m, n), dt))
    if mode == "q":
        outs = [flat(BF16)]
    elif mode == "k":
        outs = [(pl.BlockSpec((tm, K_ROWS, HEAD_C), lambda i: (i, 0, 0)),
                 jax.ShapeDtypeStruct((m, K_ROWS, HEAD_C), F32)), flat(BF16)]
    elif mode == "v":
        outs = [flat(F32), flat(BF16)]
    else:
        outs = [flat(F32)]
    return pl.pallas_call(
        functools.partial(_cproj_kernel, mode=mode),
        grid=(m // tm,),
        in_specs=[pl.BlockSpec((tm, k), lambda i: (i, 0)), pl.BlockSpec((k, n), lambda i: (0, 0))],
        out_specs=[o[0] for o in outs],
        out_shape=[o[1] for o in outs],
        name="cproj_" + mode,
        compiler_params=_cparams(("parallel",)),
    )(xb, w)


def _mixa_kernel(u_ref, v_ref, g_ref, lng_ref, lnb_ref, wm_ref, bias_ref, oa_ref, vn_ref, *, chunk, n_chunks):
    v = v_ref[...]
    mu = jnp.mean(v, -1, keepdims=True)
    vc = v - mu
    var = jnp.mean(vc * vc, -1, keepdims=True)
    vn = vc * lax.rsqrt(var + LN_EPS) * lng_ref[...] + lnb_ref[...]
    vn_ref[...] = vn
    vnb = vn.astype(BF16)
    for c in range(n_chunks):
        rows = slice(c * chunk, (c + 1) * chunk)
        parts = [_dot(wm_ref[g], vnb[rows, g * C_A:(g + 1) * C_A]) for g in range(G_A)]
        sg = jnp.concatenate(parts, axis=-1) + bias_ref[...]
        oa_ref[rows, :] = u_ref[rows, :] * sg * _silu(g_ref[rows, :])


def _mix_a(proj, lng, lnb, wm, bias, chunk, n_chunks):
    m = proj.shape[0]
    tm = chunk * n_chunks
    blk = lambda j: pl.BlockSpec((tm, D_A), lambda i, j=j: (i, j))
    full = lambda a: pl.BlockSpec(a.shape, lambda i: (0,) * a.ndim)
    return pl.pallas_call(
        functools.partial(_mixa_kernel, chunk=chunk, n_chunks=n_chunks),
        grid=(m // tm,),
        in_specs=[blk(0), blk(1), blk(2), full(lng), full(lnb), full(wm), full(bias)],
        out_specs=[pl.BlockSpec((tm, D_A), lambda i: (i, 0))] * 2,
        out_shape=[jax.ShapeDtypeStruct((m, D_A), F32)] * 2,
        name="mixer_a",
        compiler_params=_cparams(("parallel",)),
    )(proj, proj, proj, lng, lnb, wm, bias)


def _rwkv_kernel(r_ref, k_ref, v_ref, g_ref, wa_ref, sh0_ref, s0_ref, mu_ref, w0_ref, w2_ref, a0_ref,
                 a2_ref, kkw_ref, kaw_ref, rkw_ref, lng_ref, lnb_ref, ob_ref, sout_ref, prev_scr, st_scr,
                 *, prec):
    L = SCAN_L
    c = pl.program_id(1)

    @pl.when(c == 0)
    def _():
        prev_scr[...] = sh0_ref[0]
        st_scr[...] = s0_ref[0]

    def shifted(ref, idx, width):
        x = ref[...]
        first = lax.broadcasted_iota(jnp.int32, (L, width), 0) == 0
        prev = jnp.where(first, prev_scr[idx:idx + 1, :width], pltpu.roll(x, 1, 0))
        prev_scr[idx:idx + 1, :width] = x[L - 1:L, :]
        return x + (prev - x) * mu_ref[idx:idx + 1, :width]

    r = shifted(r_ref, 0, D_B)
    k = shifted(k_ref, 1, D_B)
    v = shifted(v_ref, 2, D_B)
    wa = shifted(wa_ref, 3, R_W + R_A)

    z = w0_ref[...] + _dot(jnp.tanh(wa).astype(BF16), w2_ref[...])
    nz = -z
    softplus = jnp.maximum(nz, 0.0) + jnp.log(1.0 + jnp.exp(-jnp.abs(nz)))
    log_decay = -jnp.exp(-softplus - 0.5)
    a = jax.nn.sigmoid(a0_ref[...] + _dot(wa.astype(BF16), a2_ref[...]))

    row4 = lax.broadcasted_iota(jnp.int32, (QUAD, QUAD), 0)
    col4 = lax.broadcasted_iota(jnp.int32, (QUAD, QUAD), 1)
    same_head = (row4 // HEAD_B) == (col4 // HEAD_B)
    ones_head = same_head.astype(F32)
    ones_bf = same_head.astype(BF16)
    fast = prec is None
    mask_bd = ones_bf if fast else ones_head

    def cast(x):
        return x.astype(BF16) if fast else x

    def split2(x):
        hi = x.astype(BF16)
        return jnp.concatenate([hi, (x - hi.astype(F32)).astype(BF16)], axis=0)

    def head_sum(x):
        parts = []
        for q in range(N_QUAD):
            s = _dot(split2(x[:, q * QUAD:(q + 1) * QUAD]), ones_bf)
            parts.append(s[:L] + s[L:])
        return jnp.concatenate(parts, axis=-1)

    kk = k * kkw_ref[...]
    kk = kk / jnp.maximum(jnp.sqrt(head_sum(kk * kk)), 1e-12)
    k_mod = k * (1.0 + (a - 1.0) * kaw_ref[...])
    bonus = head_sum(r * k_mod * rkw_ref[...]) * v

    tl = lax.broadcasted_iota(jnp.int32, (L, L), 0)
    sl = lax.broadcasted_iota(jnp.int32, (L, L), 1)
    cum = _dot((sl <= tl).astype(F32), log_decay, lax.Precision.HIGHEST)
    cum_last = cum[L - 1:L, :]
    p_in = jnp.exp(cum)
    p_inv = jnp.exp(-cum)
    p_tail = jnp.exp(cum_last - cum)
    beta = kk * a
    kap = cast(kk * jnp.exp(cum - log_decay))
    bet = cast(beta * p_inv)
    kt = cast(k_mod * p_inv)
    rt = cast(r * p_in)
    kt_tail = cast(k_mod * p_tail)
    bet_tail = cast(-(beta * p_tail))
    vc = cast(v)
    p_last = jnp.exp(cum_last)

    tq = lax.broadcasted_iota(jnp.int32, (L, QUAD), 0)
    sq = lax.broadcasted_iota(jnp.int32, (L, QUAD), 1) % HEAD_B
    strict = sq < tq
    incl = sq <= tq
    eye_c = (sq == tq).astype(F32)
    blk16 = (tq // 16) == (sq // 16)
    blk32 = (tq // 32) == (sq // 32)
    strict16 = strict & blk16
    off32 = strict & blk32 & jnp.logical_not(blk16)
    off64 = strict & jnp.logical_not(blk32)

    def bd(x):
        return jnp.concatenate([x] * (QUAD // L), axis=0) * mask_bd

    def mm(x, y_bd):
        return _dot(cast(x), y_bd, prec)

    quads = range(N_QUAD)
    lanes = [slice(q * QUAD, (q + 1) * QUAD) for q in quads]
    st = [st_scr[q] for q in quads]
    kr = [jnp.concatenate([kap[:, lanes[q]], rt[:, lanes[q]]], axis=0) for q in quads]
    g_b = [_dot_nt(kr[q], bd(bet[:, lanes[q]]), prec) for q in quads]
    g_k = [_dot_nt(kr[q], bd(kt[:, lanes[q]]), prec) for q in quads]
    g_s = [_dot_nt(kr[q], cast(st[q]), prec) for q in quads]
    bd_v = [bd(vc[:, lanes[q]]) for q in quads]

    pw = [jnp.where(strict16, -g_b[q][:L], 0.0) for q in quads]
    t_inv = [eye_c + pw[q] for q in quads]
    for _ in range(3):
        pw = [mm(pw[q], bd(cast(pw[q]))) for q in quads]
        t_inv = [t_inv[q] + mm(t_inv[q], bd(cast(pw[q]))) for q in quads]
    rhs = [g_s[q][:L] + mm(jnp.where(strict, g_k[q][:L], 0.0), bd_v[q]) for q in quads]
    for off_mask in (off32, off64):
        low = [mm(jnp.where(off_mask, g_b[q][:L], 0.0), bd(cast(t_inv[q]))) for q in quads]
        t_inv = [t_inv[q] - mm(t_inv[q], bd(cast(low[q]))) for q in quads]

    u_c = [cast(mm(t_inv[q], bd(cast(rhs[q])))) for q in quads]
    ys = [g_s[q][L:] + mm(jnp.where(incl, g_k[q][L:], 0.0), bd_v[q])
          - mm(jnp.where(incl, g_b[q][L:], 0.0), bd(u_c[q])) for q in quads]
    for q in quads:
        upd = _dot_tn(jnp.concatenate([vc[:, lanes[q]], u_c[q]], axis=0),
                      jnp.concatenate([kt_tail[:, lanes[q]], bet_tail[:, lanes[q]]], axis=0), prec)
        st_scr[q] = (st[q] * p_last[:, lanes[q]] + upd) * ones_head

    y = jnp.concatenate(ys, axis=-1)
    mu_y = head_sum(y) * (1.0 / HEAD_B)
    yc = y - mu_y
    var_y = head_sum(yc * yc) * (1.0 / HEAD_B)
    yn = yc * lax.rsqrt(var_y + GN_EPS_B) * lng_ref[...] + lnb_ref[...]
    ob_ref[...] = (yn + bonus) * _silu(g_ref[...])
    sout_ref[0] = st_scr[...]


def _rwkv(proj, wa, sh0, s0, prm, bsz, t, prec):
    nc = t // SCAN_L
    col = lambda j: pl.BlockSpec((SCAN_L, D_B), lambda b, c, j=j: (b * nc + c, j))
    full = lambda a: pl.BlockSpec(a.shape, lambda b, c: (0,) * a.ndim)
    per_b = lambda a: pl.BlockSpec((1,) + a.shape[1:], lambda b, c: (b,) + (0,) * (a.ndim - 1))
    return pl.pallas_call(
        functools.partial(_rwkv_kernel, prec=prec),
        grid=(bsz, nc),
        in_specs=[col(3), col(4), col(5), col(6),
                  pl.BlockSpec((SCAN_L, R_W + R_A), lambda b, c: (b * nc + c, 0)),
                  per_b(sh0), per_b(s0)] + [full(p) for p in prm],
        out_specs=[pl.BlockSpec((SCAN_L, D_B), lambda b, c: (b * nc + c, 0)), per_b(s0)],
        out_shape=[jax.ShapeDtypeStruct((bsz * t, D_B), F32), jax.ShapeDtypeStruct(s0.shape, F32)],
        scratch_shapes=[pltpu.VMEM((4, D_B), F32), pltpu.VMEM((N_QUAD, QUAD, QUAD), F32)],
        name="rwkv_scan",
        compiler_params=_cparams(("parallel", "arbitrary")),
    )(proj, proj, proj, proj, wa, sh0, s0, *prm)


def _outproj_ln_kernel(a_ref, b_ref, x_ref, w1_ref, w2_ref, g_ref, bias_ref, o_ref, *maybe_bf_ref):
    acc = _dot(a_ref[...].astype(BF16), w1_ref[...]) + _dot(b_ref[...].astype(BF16), w2_ref[...])
    y = DEEPNORM_ALPHA * x_ref[...] + acc
    mu = jnp.mean(y, -1, keepdims=True)
    yc = y - mu
    var = jnp.mean(yc * yc, -1, keepdims=True)
    out = yc * lax.rsqrt(var + LN_EPS) * g_ref[...] + bias_ref[...]
    o_ref[...] = out
    for ref in maybe_bf_ref:
        ref[...] = out.astype(BF16)


def _outproj_ln(a, a_col, b, b_col, x, w1, w2, g, bias, tm, want_bf16):
    m = x.shape[0]
    half = D_MODEL // 2
    full = lambda arr: pl.BlockSpec(arr.shape, lambda i: (0,) * arr.ndim)
    row = pl.BlockSpec((tm, D_MODEL), lambda i: (i, 0))
    n_out = 2 if want_bf16 else 1
    return pl.pallas_call(
        _outproj_ln_kernel,
        grid=(m // tm,),
        in_specs=[pl.BlockSpec((tm, half), lambda i: (i, a_col)),
                  pl.BlockSpec((tm, half), lambda i: (i, b_col)),
                  row, full(w1), full(w2), full(g), full(bias)],
        out_specs=[row] * n_out,
        out_shape=[jax.ShapeDtypeStruct((m, D_MODEL), F32), jax.ShapeDtypeStruct((m, D_MODEL), BF16)][:n_out],
        name="outproj_ln",
        compiler_params=_cparams(("parallel",)),
    )(a, b, x, w1, w2, g, bias)


def _lambda(lq1_ref, lk1_ref, lq2_ref, lk2_ref, lam_init):
    s1 = jnp.sum(lq1_ref[...] * lk1_ref[...], -1, keepdims=True)
    s2 = jnp.sum(lq2_ref[...] * lk2_ref[...], -1, keepdims=True)
    return jnp.exp(s1) - jnp.exp(s2) + lam_init


def _attn_finish(o, g, subg_ref, lam_init):
    o = o * lax.rsqrt(jnp.mean(o * o, -1, keepdims=True) + SUBLN_EPS) * subg_ref[...]
    return o * (1.0 - lam_init) * _silu(g)


ATT_ROWS = 128
LANES = 128


def _attn_prompt_kernel(q_ref, k_ref, v_ref, g_ref, lq1_ref, lk1_ref, lq2_ref, lk2_ref, subg_ref, o_ref,
                        m0_ref, m1_ref, l0_ref, l1_ref, acc0_ref, acc1_ref, *, tq, lam_init):
    qi = pl.program_id(2)
    lam = _lambda(lq1_ref, lk1_ref, lq2_ref, lk2_ref, lam_init)
    ms = (m0_ref, m1_ref)
    ls = (l0_ref, l1_ref)
    accs = (acc0_ref, acc1_ref)
    n_sub = tq // ATT_ROWS
    n_rep = tq // LANES
    for j in range(2):
        ms[j][...] = jnp.full((tq, LANES), -jnp.inf, F32)
        ls[j][...] = jnp.zeros((tq, LANES), F32)
        accs[j][...] = jnp.zeros((tq, 2 * HEAD_C), F32)

    def visit(off, masked):
        kblk = k_ref[pl.ds(off, tq), :]
        vblk = v_ref[pl.ds(off, tq), :]
        chains = [(r, j) for r in range(n_sub) for j in range(2)]
        scores = {}
        for r, j in chains:
            rows = pl.ds(r * ATT_ROWS, ATT_ROWS)
            cols = slice(j * HEAD_C, (j + 1) * HEAD_C)
            scores[r, j] = _dot_nt(q_ref[rows, cols], kblk[:, cols])
        for r, j in chains:
            rows = pl.ds(r * ATT_ROWS, ATT_ROWS)
            s = scores[r, j]
            if masked:
                rq = (lax.broadcasted_iota(jnp.int32, (ATT_ROWS, tq), 0) + r * ATT_ROWS) // ATT_CHUNK
                ck = lax.broadcasted_iota(jnp.int32, (ATT_ROWS, tq), 1) // ATT_CHUNK
                s = jnp.where(ck <= rq, s, -jnp.inf)
            m_old = ms[j][rows, :]
            m_new = jnp.maximum(m_old, jnp.max(s, -1, keepdims=True))
            alpha = jnp.exp2(m_old - m_new)
            p = jnp.exp2(s - jnp.concatenate([m_new] * n_rep, axis=1))
            part = p[:, :LANES]
            for c in range(1, n_rep):
                part = part + p[:, c * LANES:(c + 1) * LANES]
            ls[j][rows, :] = alpha * ls[j][rows, :] + part
            accs[j][rows, :] = jnp.concatenate([alpha] * 2, axis=1) * accs[j][rows, :] + _dot(p.astype(BF16), vblk)
            ms[j][rows, :] = m_new

    def body(j, carry):
        visit(pl.multiple_of(j * tq, tq), False)
        return carry

    lax.fori_loop(0, qi, body, 0)
    visit(pl.multiple_of(qi * tq, tq), True)
    l0 = jnp.sum(l0_ref[...], -1, keepdims=True)
    l1 = jnp.sum(l1_ref[...], -1, keepdims=True)
    o = acc0_ref[...] / l0 - lam * (acc1_ref[...] / l1)
    o_ref[...] = _attn_finish(o, g_ref[...], subg_ref, lam_init)


def _attn_prompt(q, k, v, g, lam_prm, subg, bsz, t, lam_init, tq):
    nq = t // tq
    hw = 2 * HEAD_C
    full = lambda a: pl.BlockSpec(a.shape, lambda b, h, i: (0,) * a.ndim)
    tile = pl.BlockSpec((tq, hw), lambda b, h, i: (b * nq + i, h))
    whole = pl.BlockSpec((t, hw), lambda b, h, i: (b, h))
    return pl.pallas_call(
        functools.partial(_attn_prompt_kernel, tq=tq, lam_init=lam_init),
        grid=(bsz, H_C, nq),
        in_specs=[tile, whole, whole, tile] + [full(p) for p in lam_prm] + [full(subg)],
        out_specs=tile,
        out_shape=jax.ShapeDtypeStruct((bsz * t, D_C), F32),
        scratch_shapes=[pltpu.VMEM((tq, LANES), F32)] * 4 + [pltpu.VMEM((tq, hw), F32)] * 2,
        name="attn_prompt",
        compiler_params=_cparams(("parallel", "parallel", "arbitrary")),
    )(q, k, v, g, *lam_prm, subg)


def _attn_sample_kernel(q_ref, kn_ref, vn_ref, g_ref, kc_ref, vc_ref, lq1_ref, lk1_ref, lq2_ref, lk2_ref,
                        subg_ref, o_ref, *, lam_init):
    lam = _lambda(lq1_ref, lk1_ref, lq2_ref, lk2_ref, lam_init)
    q = q_ref[...]
    kc = kc_ref[...].astype(BF16)
    kn = kn_ref[...]
    probs = []
    for j in range(2):
        cols = slice(j * HEAD_C, (j + 1) * HEAD_C)
        sc = _dot_nt(q[:, cols], kc[:, cols])
        sn = _dot_nt(q[:, cols], kn[:, cols])
        m = jnp.maximum(jnp.max(sc, -1, keepdims=True), jnp.max(sn, -1, keepdims=True))
        pc = jnp.exp2(sc - m)
        pn = jnp.exp2(sn - m)
        inv = 1.0 / (jnp.sum(pc, -1, keepdims=True) + jnp.sum(pn, -1, keepdims=True))
        probs.append((pc * inv, pn * inv))
    ac = probs[0][0] - lam * probs[1][0]
    an = probs[0][1] - lam * probs[1][1]
    o = _dot(ac.astype(BF16), vc_ref[...].astype(BF16)) + _dot(an.astype(BF16), vn_ref[...])
    o_ref[...] = _attn_finish(o, g_ref[...], subg_ref, lam_init)


def _attn_sample(q, kn, vn, g, kc, vc, lam_prm, subg, bsz, t, past, lam_init):
    hw = 2 * HEAD_C
    full = lambda a: pl.BlockSpec(a.shape, lambda b, h: (0,) * a.ndim)
    new = pl.BlockSpec((t, hw), lambda b, h: (b, h))
    cache = pl.BlockSpec((past, hw), lambda b, h: (b, h))
    return pl.pallas_call(
        functools.partial(_attn_sample_kernel, lam_init=lam_init),
        grid=(bsz, H_C),
        in_specs=[new, new, new, new, cache, cache] + [full(p) for p in lam_prm] + [full(subg)],
        out_specs=new,
        out_shape=jax.ShapeDtypeStruct((bsz * t, D_C), F32),
        name="attn_sample",
        compiler_params=_cparams(("parallel", "parallel")),
    )(q, kn, vn, g, kc, vc, *lam_prm, subg)


def _row(p):
    return p.reshape(1, -1).astype(F32)


def _pad_rows(x, width):
    lead = x.shape[:-1]
    main = x[..., :3 * width].reshape(lead + (3, width))
    rest = jnp.pad(x[..., 3 * width:], [(0, 0)] * len(lead) + [(0, 4 * width - x.shape[-1])])
    return jnp.concatenate([main, rest[..., None, :]], axis=-2)


def _ab_params(w_in, a_ln_g, a_ln_b, a_ws, a_bs, b_mu, b_w0, b_w2, b_a0, b_a2, b_kk, b_ka, b_rk, b_lnx_g,
               b_lnx_b, w_out, ln_g, ln_b):
    split = 3 * D_A + 3 * D_B
    w_main = jnp.concatenate([w_in[:, :split], w_in[:, split + R_W + R_A:]], axis=1).astype(BF16)
    w_tail = w_in[:, split:split + R_W + R_A].astype(BF16)
    zeros = jnp.zeros((R_W, D_B), F32)
    rwkv = (_pad_rows(b_mu, D_B), _row(b_w0),
            jnp.concatenate([b_w2, zeros], 0).astype(BF16), _row(b_a0),
            jnp.concatenate([zeros, b_a2], 0).astype(BF16),
            _row(b_kk), _row(b_ka), _row(b_rk), _row(b_lnx_g), _row(b_lnx_b))
    return dict(w_main=w_main, w_tail=w_tail, a_ln_g=_row(a_ln_g), a_ln_b=_row(a_ln_b), a_ws=a_ws, a_bs=a_bs,
                rwkv=rwkv, w_out=w_out.astype(BF16), ln_g=_row(ln_g), ln_b=_row(ln_b))


def _state_to_quads(wkv):
    b = wkv.shape[0]
    x = wkv.reshape(b, N_QUAD, 4, HEAD_B, HEAD_B).astype(F32)
    eye = jnp.eye(4, dtype=F32)
    return (x[:, :, :, :, None, :] * eye[None, None, :, None, :, None]).reshape(b, N_QUAD, QUAD, QUAD)


def _quads_to_state(st):
    b = st.shape[0]
    x = st.reshape(b, N_QUAD, 4, HEAD_B, 4, HEAD_B)
    x = jnp.diagonal(x, axis1=2, axis2=4)
    return jnp.moveaxis(x, -1, 2).reshape(b, H_B, HEAD_B, HEAD_B)


def _ab_layer(x, shift_prev, wkv0, prm, scan_prec):
    bsz, t, _ = x.shape
    m = bsz * t
    x2 = x.reshape(m, D_MODEL)
    tm = min(m, 1024)
    proj = _matmul(x2, prm["w_main"], tm, D_A)
    wa = _matmul(x2, prm["w_tail"], tm, R_W + R_A)

    chunk = min(t, MLP_CHUNK)
    n_chunks = 2 if t >= 2 * MLP_CHUNK else 1
    mask = jnp.tril(jnp.ones((chunk, chunk), dtype=bool))
    wm = jnp.where(mask, prm["a_ws"][:, :chunk, :chunk], 0).astype(BF16)
    bias = jnp.repeat(jnp.swapaxes(prm["a_bs"][:, :chunk], 0, 1), C_A, axis=1).astype(F32)
    out_a, v_n = _mix_a(proj, prm["a_ln_g"], prm["a_ln_b"], wm, bias, chunk, n_chunks)

    sh0 = _pad_rows(shift_prev.astype(F32), D_B)
    out_b, st_new = _rwkv(proj, wa, sh0, _state_to_quads(wkv0), prm["rwkv"], bsz, t, scan_prec)

    w_out = prm["w_out"]
    x_new, x_new_bf = _outproj_ln(out_a, 0, out_b, 0, x2, w_out[:D_A], w_out[D_A:], prm["ln_g"], prm["ln_b"],
                                  min(m, 256), True)
    last = proj.reshape(bsz, t, -1)[:, -1]
    shift_new = jnp.concatenate([last[:, 3 * D_A:3 * D_A + 3 * D_B], wa.reshape(bsz, t, -1)[:, -1]], axis=-1)
    return (x_new, x_new_bf), shift_new, _quads_to_state(st_new), v_n.reshape(bsz, t, D_A)


def _c_layer(x2, xb, bsz, t, k_cache, v_cache, lam_init, w_in, lam_prm, subg, w_out, ln_g, ln_b):
    m = bsz * t
    tm = min(m, 512)
    wq, wk, wv, wg = (w_in[:, i * D_C:(i + 1) * D_C].astype(BF16) for i in range(4))
    q, = _cproj(xb, wq, "q", tm)
    k_leaf, kb = _cproj(xb, wk, "k", tm)
    v_leaf, vb = _cproj(xb, wv, "v", tm)
    g, = _cproj(xb, wg, "g", tm)
    if k_cache is None:
        o = _attn_prompt(q, kb, vb, g, lam_prm, subg, bsz, t, lam_init, 512)
    else:
        past = k_cache.shape[1]
        o = _attn_sample(q, kb, vb, g, k_cache.reshape(bsz * past, D_C), v_cache.reshape(bsz * past, D_C),
                         lam_prm, subg, bsz, t, past, lam_init)
    x_new, = _outproj_ln(o, 0, o, 1, x2, w_out[:D_C // 2], w_out[D_C // 2:], ln_g, ln_b, min(m, 256), False)
    k_new = k_leaf.reshape(bsz, t, H_C, 2, HEAD_C)
    v_new = v_leaf.reshape(bsz, t, H_C, 2 * HEAD_C)
    return x_new.reshape(bsz, t, D_MODEL), k_new, v_new


def kernel(x_prompt, x_sample, state_b_shift, state_b_wkv, cache_c_k, cache_c_v, ab_w_in, ab_a_ln_g, ab_a_ln_b, ab_a_ws, ab_a_bs, ab_b_mu, ab_b_w0, ab_b_w2, ab_b_a0, ab_b_a2, ab_b_kk, ab_b_ka, ab_b_rk, ab_b_lnx_g, ab_b_lnx_b, ab_w_out, ab_ln_g, ab_ln_b, c_w_in, c_lam_q1, c_lam_k1, c_lam_q2, c_lam_k2, c_subln_g, c_w_out, c_ln_g, c_ln_b):
    scan_prec = None
    bp, tp = x_prompt.shape[:2]
    bs, ts = x_sample.shape[:2]
    x_p, x_s = x_prompt, x_sample
    xp_pair = xs_pair = None
    sh_p_l, wkv_p_l, sh_s_l, wkv_s_l, va_s_l = [], [], [], [], []
    kp_l, vp_l, ks_l, vs_l = [], [], [], []
    for li in range(DEPTH):
        j = li // 2
        if li % 2 == 0:
            prm = _ab_params(ab_w_in[j], ab_a_ln_g[j], ab_a_ln_b[j], ab_a_ws[j], ab_a_bs[j], ab_b_mu[j],
                             ab_b_w0[j], ab_b_w2[j], ab_b_a0[j], ab_b_a2[j], ab_b_kk[j], ab_b_ka[j],
                             ab_b_rk[j], ab_b_lnx_g[j], ab_b_lnx_b[j], ab_w_out[j], ab_ln_g[j], ab_ln_b[j])
            shift0 = jnp.zeros((bp, D_B_SHIFT), F32)
            wkv0 = jnp.zeros((bp, H_B, HEAD_B, HEAD_B), F32)
            xp_pair, sh_p, wkv_p, _ = _ab_layer(x_p, shift0, wkv0, prm, scan_prec)
            xs_pair, sh_s, wkv_s, va_s = _ab_layer(x_s, state_b_shift[j], state_b_wkv[j], prm, scan_prec)
            sh_p_l.append(sh_p)
            wkv_p_l.append(wkv_p)
            sh_s_l.append(sh_s)
            wkv_s_l.append(wkv_s)
            va_s_l.append(va_s)
        else:
            lam_init = 0.8 - 0.6 * math.exp(-0.3 * li)
            lam_prm = (_row(c_lam_q1[j]), _row(c_lam_k1[j]), _row(c_lam_q2[j]), _row(c_lam_k2[j]))
            args = (lam_init, c_w_in[j], lam_prm, _row(c_subln_g[j]), c_w_out[j].astype(BF16), _row(c_ln_g[j]),
                    _row(c_ln_b[j]))
            x_p, k_p, v_p = _c_layer(*xp_pair, bp, tp, None, None, *args)
            x_s, k_s, v_s = _c_layer(*xs_pair, bs, ts, cache_c_k[j], cache_c_v[j], *args)
            kp_l.append(k_p)
            vp_l.append(v_p)
            ks_l.append(k_s)
            vs_l.append(v_s)
    return (x_p, x_s, jnp.stack(sh_p_l), jnp.stack(wkv_p_l), jnp.stack(sh_s_l), jnp.stack(wkv_s_l),
            jnp.stack(va_s_l), jnp.stack(kp_l), jnp.stack(vp_l), jnp.stack(ks_l), jnp.stack(vs_l))
```

```python
import functools
import math

import jax
import jax.numpy as jnp
from jax import lax
from jax.experimental import pallas as pl
from jax.experimental.pallas import tpu as pltpu

F32 = jnp.float32
BF16 = jnp.bfloat16

D_MODEL = 2048
DEPTH = 2
D_A = 1024
G_A = 4
C_A = D_A // G_A
MLP_CHUNK = 128
D_B = 1024
HEAD_B = 64
H_B = D_B // HEAD_B
R_W = 64
R_A = 64
D_B_SHIFT = 3 * D_B + R_W + R_A
HEAD_C = 128
H_C = 8
D_C = 2048
ATT_CHUNK = 64
DEEPNORM_ALPHA = (2 * DEPTH) ** 0.25
LN_EPS = 1e-5
GN_EPS_B = 64e-5
SUBLN_EPS = 1e-5

QUAD = 4 * HEAD_B
N_QUAD = D_B // QUAD
SCAN_L = 64
VMEM_LIMIT = 52 * 1024 * 1024


def _cparams(sem):
    return pltpu.CompilerParams(dimension_semantics=sem, vmem_limit_bytes=VMEM_LIMIT)


def _dot(a, b, prec=None):
    return jnp.dot(a, b, precision=prec, preferred_element_type=F32)


def _dot_nt(a, b, prec=None):
    return lax.dot_general(a, b, (((1,), (1,)), ((), ())), precision=prec, preferred_element_type=F32)


def _dot_tn(a, b, prec=None):
    return lax.dot_general(a, b, (((0,), (0,)), ((), ())), precision=prec, preferred_element_type=F32)


def _silu(x):
    return x * jax.nn.sigmoid(x)


def _mm_kernel(x_ref, w_ref, wt_ref, o_ref, ot_ref, xb_ref):
    @pl.when(pl.program_id(1) == 0)
    def _():
        xb = x_ref[...].astype(BF16)
        xb_ref[...] = xb
        ot_ref[...] = _dot(xb, wt_ref[...])

    o_ref[...] = _dot(xb_ref[...], w_ref[...])


def _matmul(x, w, w_tail, tm, tn):
    m, k = x.shape
    n = w.shape[1]
    nt = w_tail.shape[1]
    return pl.pallas_call(
        _mm_kernel,
        grid=(m // tm, n // tn),
        in_specs=[pl.BlockSpec((tm, k), lambda i, j: (i, 0)),
                  pl.BlockSpec((k, tn), lambda i, j: (0, j)),
                  pl.BlockSpec((k, nt), lambda i, j: (0, 0))],
        out_specs=[pl.BlockSpec((tm, tn), lambda i, j: (i, j)), pl.BlockSpec((tm, nt), lambda i, j: (i, 0))],
        out_shape=[jax.ShapeDtypeStruct((m, n), F32), jax.ShapeDtypeStruct((m, nt), F32)],
        scratch_shapes=[pltpu.VMEM((tm, k), BF16)],
        name="proj_matmul",
        compiler_params=_cparams(("parallel", "arbitrary")),
    )(x, w, w_tail)


QK_SCALE = (HEAD_C ** -0.5) * math.log2(math.e)
K_ROWS = 2 * H_C


def _cproj_kernel(x_ref, w_ref, *o_refs, mode):
    acc = _dot(x_ref[...], w_ref[...])
    if mode == "q":
        o_refs[0][...] = (acc * QK_SCALE).astype(BF16)
    elif mode == "k":
        leaf_ref, bf_ref = o_refs
        bf_ref[...] = acc.astype(BF16)
        for c in range(K_ROWS):
            leaf_ref[:, c, :] = acc[:, c * HEAD_C:(c + 1) * HEAD_C]
    elif mode == "v":
        leaf_ref, bf_ref = o_refs
        leaf_ref[...] = acc
        bf_ref[...] = acc.astype(BF16)
    else:
        o_refs[0][...] = acc


def _cproj(xb, w, mode, tm):
    m, k = xb.shape
    n = w.shape[1]
    flat = lambda dt: (pl.BlockSpec((tm, n), lambda i: (i, 0)), jax.ShapeDtypeStruct((m, n), dt))
    if mode == "q":
        outs = [flat(BF16)]
    elif mode == "k":
        outs = [(pl.BlockSpec((tm, K_ROWS, HEAD_C), lambda i: (i, 0, 0)),
                 jax.ShapeDtypeStruct((m, K_ROWS, HEAD_C), F32)), flat(BF16)]
    elif mode == "v":
        outs = [flat(F32), flat(BF16)]
    else:
        outs = [flat(F32)]
    return pl.pallas_call(
        functools.partial(_cproj_kernel, mode=mode),
        grid=(m // tm,),
        in_specs=[pl.BlockSpec((tm, k), lambda i: (i, 0)), pl.BlockSpec((k, n), lambda i: (0, 0))],
        out_specs=[o[0] for o in outs],
        out_shape=[o[1] for o in outs],
        name="cproj_" + mode,
        compiler_params=_cparams(("parallel",)),
    )(xb, w)


def _mixa_kernel(u_ref, v_ref, g_ref, lng_ref, lnb_ref, wm_ref, bias_ref, oa_ref, *maybe_vn_ref, chunk, n_chunks):
    v = v_ref[...]
    mu = jnp.mean(v, -1, keepdims=True)
    vc = v - mu
    var = jnp.mean(vc * vc, -1, keepdims=True)
    vn = vc * lax.rsqrt(var + LN_EPS) * lng_ref[...] + lnb_ref[...]
    for ref in maybe_vn_ref:
        ref[...] = vn
    vnb = vn.astype(BF16)
    for c in range(n_chunks):
        rows = slice(c * chunk, (c + 1) * chunk)
        parts = [_dot(wm_ref[g], vnb[rows, g * C_A:(g + 1) * C_A]) for g in range(G_A)]
        sg = jnp.concatenate(parts, axis=-1) + bias_ref[...]
        oa_ref[rows, :] = u_ref[rows, :] * sg * _silu(g_ref[rows, :])


def _mix_a(proj, lng, lnb, wm, bias, chunk, n_chunks, want_vn):
    m = proj.shape[0]
    tm = chunk * n_chunks
    blk = lambda j: pl.BlockSpec((tm, D_A), lambda i, j=j: (i, j))
    full = lambda a: pl.BlockSpec(a.shape, lambda i: (0,) * a.ndim)
    n_out = 2 if want_vn else 1
    return pl.pallas_call(
        functools.partial(_mixa_kernel, chunk=chunk, n_chunks=n_chunks),
        grid=(m // tm,),
        in_specs=[blk(0), blk(1), blk(2), full(lng), full(lnb), full(wm), full(bias)],
        out_specs=[pl.BlockSpec((tm, D_A), lambda i: (i, 0))] * n_out,
        out_shape=[jax.ShapeDtypeStruct((m, D_A), F32)] * n_out,
        name="mixer_a",
        compiler_params=_cparams(("parallel",)),
    )(proj, proj, proj, lng, lnb, wm, bias)


def _rwkv_kernel(r_ref, k_ref, v_ref, g_ref, wa_ref, sh0_ref, s0_ref, mu_ref, w0_ref, w2_ref, a0_ref,
                 a2_ref, kkw_ref, kaw_ref, rkw_ref, lng_ref, lnb_ref, ob_ref, sout_ref, prev_scr, st_scr,
                 *, prec):
    L = SCAN_L
    nb = r_ref.shape[0]
    c = pl.program_id(1)

    @pl.when(c == 0)
    def _():
        prev_scr[...] = sh0_ref[...]
        st_scr[...] = s0_ref[...]

    row4 = lax.broadcasted_iota(jnp.int32, (QUAD, QUAD), 0)
    col4 = lax.broadcasted_iota(jnp.int32, (QUAD, QUAD), 1)
    same_head = (row4 // HEAD_B) == (col4 // HEAD_B)
    ones_head = same_head.astype(F32)
    ones_bf = same_head.astype(BF16)
    fast = prec is None
    mask_bd = ones_bf if fast else ones_head

    def cast(x):
        return x.astype(BF16) if fast else x

    def split2(x):
        hi = x.astype(BF16)
        return jnp.concatenate([hi, (x - hi.astype(F32)).astype(BF16)], axis=0)

    def head_sum(x):
        parts = []
        for q in range(N_QUAD):
            s = _dot(split2(x[:, q * QUAD:(q + 1) * QUAD]), ones_bf)
            parts.append(s[:L] + s[L:])
        return jnp.concatenate(parts, axis=-1)

    tl = lax.broadcasted_iota(jnp.int32, (L, L), 0)
    sl = lax.broadcasted_iota(jnp.int32, (L, L), 1)
    tri_incl = (sl <= tl).astype(F32)

    def prepare(b):
        def shifted(ref, idx, width):
            x = ref[b]
            first = lax.broadcasted_iota(jnp.int32, (L, width), 0) == 0
            prev = jnp.where(first, prev_scr[b, idx:idx + 1, :width], pltpu.roll(x, 1, 0))
            prev_scr[b, idx:idx + 1, :width] = x[L - 1:L, :]
            return x + (prev - x) * mu_ref[idx:idx + 1, :width]

        r = shifted(r_ref, 0, D_B)
        k = shifted(k_ref, 1, D_B)
        v = shifted(v_ref, 2, D_B)
        wa = shifted(wa_ref, 3, R_W + R_A)

        z = w0_ref[...] + _dot(jnp.tanh(wa).astype(BF16), w2_ref[...])
        nz = -z
        softplus = jnp.maximum(nz, 0.0) + jnp.log(1.0 + jnp.exp(-jnp.abs(nz)))
        log_decay = -jnp.exp(-softplus - 0.5)
        a = jax.nn.sigmoid(a0_ref[...] + _dot(wa.astype(BF16), a2_ref[...]))

        kk = k * kkw_ref[...]
        kk = kk / jnp.maximum(jnp.sqrt(head_sum(kk * kk)), 1e-12)
        k_mod = k * (1.0 + (a - 1.0) * kaw_ref[...])
        bonus = head_sum(r * k_mod * rkw_ref[...]) * v

        cum = _dot(tri_incl, log_decay, lax.Precision.HIGHEST)
        cum_last = cum[L - 1:L, :]
        p_inv = jnp.exp(-cum)
        p_tail = jnp.exp(cum_last - cum)
        beta = kk * a
        return dict(
            kap=cast(kk * jnp.exp(cum - log_decay)), bet=cast(beta * p_inv), kt=cast(k_mod * p_inv),
            rt=cast(r * jnp.exp(cum)), kt_tail=cast(k_mod * p_tail), bet_tail=cast(-(beta * p_tail)),
            vc=cast(v), p_last=jnp.exp(cum_last), bonus=bonus)

    tq = lax.broadcasted_iota(jnp.int32, (L, QUAD), 0)
    sq = lax.broadcasted_iota(jnp.int32, (L, QUAD), 1) % HEAD_B
    strict = sq < tq
    incl = sq <= tq
    eye_c = (sq == tq).astype(F32)
    blk16 = (tq // 16) == (sq // 16)
    blk32 = (tq // 32) == (sq // 32)
    strict16 = strict & blk16
    off32 = strict & blk32 & jnp.logical_not(blk16)
    off64 = strict & jnp.logical_not(blk32)

    def bd(x):
        return jnp.concatenate([x] * (QUAD // L), axis=0) * mask_bd

    def mm(x, y_bd):
        return _dot(cast(x), y_bd, prec)

    pre = [prepare(b) for b in range(nb)]

    chains = [(b, q) for b in range(nb) for q in range(N_QUAD)]
    n = range(len(chains))
    lanes = [slice(q * QUAD, (q + 1) * QUAD) for _, q in chains]
    sub = lambda name, i: pre[chains[i][0]][name][:, lanes[i]]
    st = [st_scr[b, q] for b, q in chains]
    kr = [jnp.concatenate([sub("kap", i), sub("rt", i)], axis=0) for i in n]
    g_b = [_dot_nt(kr[i], bd(sub("bet", i)), prec) for i in n]
    g_k = [_dot_nt(kr[i], bd(sub("kt", i)), prec) for i in n]
    g_s = [_dot_nt(kr[i], cast(st[i]), prec) for i in n]
    bd_v = [bd(sub("vc", i)) for i in n]

    pw = [jnp.where(strict16, -g_b[i][:L], 0.0) for i in n]
    t_inv = [eye_c + pw[i] for i in n]
    for _ in range(3):
        pw = [mm(pw[i], bd(cast(pw[i]))) for i in n]
        t_inv = [t_inv[i] + mm(t_inv[i], bd(cast(pw[i]))) for i in n]
    rhs = [g_s[i][:L] + mm(jnp.where(strict, g_k[i][:L], 0.0), bd_v[i]) for i in n]
    for off_mask in (off32, off64):
        low = [mm(jnp.where(off_mask, g_b[i][:L], 0.0), bd(cast(t_inv[i]))) for i in n]
        t_inv = [t_inv[i] - mm(t_inv[i], bd(cast(low[i]))) for i in n]

    u_c = [cast(mm(t_inv[i], bd(cast(rhs[i])))) for i in n]
    ys = [g_s[i][L:] + mm(jnp.where(incl, g_k[i][L:], 0.0), bd_v[i])
          - mm(jnp.where(incl, g_b[i][L:], 0.0), bd(u_c[i])) for i in n]
    for i in n:
        b, q = chains[i]
        upd = _dot_tn(jnp.concatenate([sub("vc", i), u_c[i]], axis=0),
                      jnp.concatenate([sub("kt_tail", i), sub("bet_tail", i)], axis=0), prec)
        new_st = (st[i] * sub("p_last", i) + upd) * ones_head
        st_scr[b, q] = new_st
        sout_ref[b, q] = new_st

    for b in range(nb):
        y = jnp.concatenate(ys[b * N_QUAD:(b + 1) * N_QUAD], axis=-1)
        mu_y = head_sum(y) * (1.0 / HEAD_B)
        yc = y - mu_y
        var_y = head_sum(yc * yc) * (1.0 / HEAD_B)
        yn = yc * lax.rsqrt(var_y + GN_EPS_B) * lng_ref[...] + lnb_ref[...]
        ob_ref[b] = (yn + pre[b]["bonus"]) * _silu(g_ref[b])


SCAN_NB = 4


def _rwkv(proj, wa, sh0, s0, prm, bsz, t, prec):
    nb = SCAN_NB
    nc = t // SCAN_L
    proj3 = proj.reshape(bsz, t, -1)
    col = lambda j: pl.BlockSpec((nb, SCAN_L, D_B), lambda b, c, j=j: (b, c, j))
    full = lambda a: pl.BlockSpec(a.shape, lambda b, c: (0,) * a.ndim)
    per_b = lambda a: pl.BlockSpec((nb,) + a.shape[1:], lambda b, c: (b,) + (0,) * (a.ndim - 1))
    return pl.pallas_call(
        functools.partial(_rwkv_kernel, prec=prec),
        grid=(bsz // nb, nc),
        in_specs=[col(3), col(4), col(5), col(6),
                  pl.BlockSpec((nb, SCAN_L, R_W + R_A), lambda b, c: (b, c, 0)),
                  per_b(sh0), per_b(s0)] + [full(p) for p in prm],
        out_specs=[pl.BlockSpec((nb, SCAN_L, D_B), lambda b, c: (b, c, 0)), per_b(s0)],
        out_shape=[jax.ShapeDtypeStruct((bsz, t, D_B), F32), jax.ShapeDtypeStruct(s0.shape, F32)],
        scratch_shapes=[pltpu.VMEM((nb, 4, D_B), F32), pltpu.VMEM((nb, N_QUAD, QUAD, QUAD), F32)],
        name="rwkv_scan",
        compiler_params=_cparams(("parallel", "arbitrary")),
    )(proj3, proj3, proj3, proj3, wa.reshape(bsz, t, -1), sh0, s0, *prm)


def _outproj_ln_kernel(a_ref, b_ref, x_ref, w1_ref, w2_ref, g_ref, bias_ref, o_ref, *maybe_bf_ref):
    acc = _dot(a_ref[...].astype(BF16), w1_ref[...]) + _dot(b_ref[...].astype(BF16), w2_ref[...])
    y = DEEPNORM_ALPHA * x_ref[...] + acc
    mu = jnp.mean(y, -1, keepdims=True)
    yc = y - mu
    var = jnp.mean(yc * yc, -1, keepdims=True)
    out = yc * lax.rsqrt(var + LN_EPS) * g_ref[...] + bias_ref[...]
    o_ref[...] = out
    for ref in maybe_bf_ref:
        ref[...] = out.astype(BF16)


def _outproj_ln(a, a_col, b, b_col, x, w1, w2, g, bias, tm, want_bf16):
    m = x.shape[0]
    half = D_MODEL // 2
    full = lambda arr: pl.BlockSpec(arr.shape, lambda i: (0,) * arr.ndim)
    row = pl.BlockSpec((tm, D_MODEL), lambda i: (i, 0))
    n_out = 2 if want_bf16 else 1
    return pl.pallas_call(
        _outproj_ln_kernel,
        grid=(m // tm,),
        in_specs=[pl.BlockSpec((tm, half), lambda i: (i, a_col)),
                  pl.BlockSpec((tm, half), lambda i: (i, b_col)),
                  row, full(w1), full(w2), full(g), full(bias)],
        out_specs=[row] * n_out,
        out_shape=[jax.ShapeDtypeStruct((m, D_MODEL), F32), jax.ShapeDtypeStruct((m, D_MODEL), BF16)][:n_out],
        name="outproj_ln",
        compiler_params=_cparams(("parallel",)),
    )(a, b, x, w1, w2, g, bias)


def _lambda(lq1_ref, lk1_ref, lq2_ref, lk2_ref, lam_init):
    s1 = jnp.sum(lq1_ref[...] * lk1_ref[...], -1, keepdims=True)
    s2 = jnp.sum(lq2_ref[...] * lk2_ref[...], -1, keepdims=True)
    return jnp.exp(s1) - jnp.exp(s2) + lam_init


def _attn_finish(o, g, subg_ref, lam_init):
    o = o * lax.rsqrt(jnp.mean(o * o, -1, keepdims=True) + SUBLN_EPS) * subg_ref[...]
    return o * (1.0 - lam_init) * _silu(g)


ATT_ROWS = 128
LANES = 128


def _attn_prompt_kernel(q_ref, k_ref, v_ref, g_ref, lq1_ref, lk1_ref, lq2_ref, lk2_ref, subg_ref, o_ref,
                        m0_ref, m1_ref, l0_ref, l1_ref, acc0_ref, acc1_ref, sa_ref, sb_ref, *, tq, lam_init):
    qi = pl.program_id(2)
    lam = _lambda(lq1_ref, lk1_ref, lq2_ref, lk2_ref, lam_init)
    ms = (m0_ref, m1_ref)
    ls = (l0_ref, l1_ref)
    accs = (acc0_ref, acc1_ref)
    n_sub = tq // ATT_ROWS
    n_rep = tq // LANES
    for j in range(2):
        ms[j][...] = jnp.full((tq, LANES), -jnp.inf, F32)
        ls[j][...] = jnp.zeros((tq, LANES), F32)
        accs[j][...] = jnp.zeros((tq, 2 * HEAD_C), F32)

    chains = [(r, j) for r in range(n_sub) for j in range(2)]

    def blk(i):
        return pl.multiple_of(i * tq, tq)

    def put_scores(r, j, off, s_ref):
        rows = pl.ds(r * ATT_ROWS, ATT_ROWS)
        cols = slice(j * HEAD_C, (j + 1) * HEAD_C)
        s_ref[j, rows, :] = _dot_nt(q_ref[rows, cols], k_ref[pl.ds(off, tq), cols])

    def consume(r, j, s_ref, off, masked):
        rows = pl.ds(r * ATT_ROWS, ATT_ROWS)
        s = s_ref[j, rows, :]
        if masked:
            rq = (lax.broadcasted_iota(jnp.int32, (ATT_ROWS, tq), 0) + r * ATT_ROWS) // ATT_CHUNK
            ck = lax.broadcasted_iota(jnp.int32, (ATT_ROWS, tq), 1) // ATT_CHUNK
            s = jnp.where(ck <= rq, s, -jnp.inf)
        m_old = ms[j][rows, :]
        m_new = jnp.maximum(m_old, jnp.max(s, -1, keepdims=True))
        alpha = jnp.exp2(m_old - m_new)
        p = jnp.exp2(s - jnp.concatenate([m_new] * n_rep, axis=1))
        part = p[:, :LANES]
        for c in range(1, n_rep):
            part = part + p[:, c * LANES:(c + 1) * LANES]
        ls[j][rows, :] = alpha * ls[j][rows, :] + part
        accs[j][rows, :] = (jnp.concatenate([alpha] * 2, axis=1) * accs[j][rows, :]
                            + _dot(p.astype(BF16), v_ref[pl.ds(off, tq), :]))
        ms[j][rows, :] = m_new

    def step(s_cur, off_cur, masked, s_next=None, off_next=None):
        for r, j in chains:
            if s_next is not None:
                put_scores(r, j, off_next, s_next)
            consume(r, j, s_cur, off_cur, masked)

    for r, j in chains:
        put_scores(r, j, blk(0), sa_ref)

    def pair(p, carry):
        step(sa_ref, blk(2 * p), False, sb_ref, blk(2 * p + 1))
        step(sb_ref, blk(2 * p + 1), False, sa_ref, blk(2 * p + 2))
        return carry

    lax.fori_loop(0, qi // 2, pair, 0)

    @pl.when(qi % 2 == 1)
    def _():
        step(sa_ref, blk(qi - 1), False, sb_ref, blk(qi))
        step(sb_ref, blk(qi), True)

    @pl.when(qi % 2 == 0)
    def _():
        step(sa_ref, blk(qi), True)

    l0 = jnp.sum(l0_ref[...], -1, keepdims=True)
    l1 = jnp.sum(l1_ref[...], -1, keepdims=True)
    o = acc0_ref[...] / l0 - lam * (acc1_ref[...] / l1)
    o_ref[...] = _attn_finish(o, g_ref[...], subg_ref, lam_init)


def _attn_prompt(q, k, v, g, lam_prm, subg, bsz, t, lam_init, tq):
    nq = t // tq
    hw = 2 * HEAD_C
    full = lambda a: pl.BlockSpec(a.shape, lambda b, h, i: (0,) * a.ndim)
    tile = pl.BlockSpec((tq, hw), lambda b, h, i: (b * nq + i, h))
    whole = pl.BlockSpec((t, hw), lambda b, h, i: (b, h))
    return pl.pallas_call(
        functools.partial(_attn_prompt_kernel, tq=tq, lam_init=lam_init),
        grid=(bsz, H_C, nq),
        in_specs=[tile, whole, whole, tile] + [full(p) for p in lam_prm] + [full(subg)],
        out_specs=tile,
        out_shape=jax.ShapeDtypeStruct((bsz * t, D_C), F32),
        scratch_shapes=([pltpu.VMEM((tq, LANES), F32)] * 4 + [pltpu.VMEM((tq, hw), F32)] * 2
                        + [pltpu.VMEM((2, tq, tq), F32)] * 2),
        name="attn_prompt",
        compiler_params=_cparams(("parallel", "parallel", "arbitrary")),
    )(q, k, v, g, *lam_prm, subg)


def _attn_sample_kernel(q_ref, kn_ref, vn_ref, g_ref, kc_ref, vc_ref, lq1_ref, lk1_ref, lq2_ref,
                        lk2_ref, subg_ref, o_ref, *, lam_init):
    lam = _lambda(lq1_ref, lk1_ref, lq2_ref, lk2_ref, lam_init)
    q = q_ref[...]
    kn = kn_ref[...]
    kc = kc_ref[...].astype(BF16)
    probs = []
    for j in range(2):
        cols = slice(j * HEAD_C, (j + 1) * HEAD_C)
        sc = _dot_nt(q[:, cols], kc[:, cols])
        sn = _dot_nt(q[:, cols], kn[:, cols])
        m = jnp.maximum(jnp.max(sc, -1, keepdims=True), jnp.max(sn, -1, keepdims=True))
        pc = jnp.exp2(sc - m)
        pn = jnp.exp2(sn - m)
        inv = 1.0 / (jnp.sum(pc, -1, keepdims=True) + jnp.sum(pn, -1, keepdims=True))
        probs.append((pc * inv, pn * inv))
    ac = probs[0][0] - lam * probs[1][0]
    an = probs[0][1] - lam * probs[1][1]
    o = _dot(ac.astype(BF16), vc_ref[...].astype(BF16)) + _dot(an.astype(BF16), vn_ref[...])
    o_ref[...] = _attn_finish(o, g_ref[...], subg_ref, lam_init)


def _attn_sample(q, kn, vn, g, kc, vc, lam_prm, subg, bsz, t, past, lam_init):
    hw = 2 * HEAD_C
    full = lambda a: pl.BlockSpec(a.shape, lambda b, h: (0,) * a.ndim)
    new = pl.BlockSpec((t, hw), lambda b, h: (b, h))
    cache = pl.BlockSpec((past, hw), lambda b, h: (b, h))
    return pl.pallas_call(
        functools.partial(_attn_sample_kernel, lam_init=lam_init),
        grid=(bsz, H_C),
        in_specs=[new, new, new, new, cache, cache] + [full(p) for p in lam_prm] + [full(subg)],
        out_specs=new,
        out_shape=jax.ShapeDtypeStruct((bsz * t, D_C), F32),
        name="attn_sample",
        compiler_params=_cparams(("parallel", "parallel")),
    )(q, kn, vn, g, kc, vc, *lam_prm, subg)


def _row(p):
    return p.reshape(1, -1).astype(F32)


def _pad_rows(x, width):
    lead = x.shape[:-1]
    main = x[..., :3 * width].reshape(lead + (3, width))
    rest = jnp.pad(x[..., 3 * width:], [(0, 0)] * len(lead) + [(0, 4 * width - x.shape[-1])])
    return jnp.concatenate([main, rest[..., None, :]], axis=-2)


def _ab_params(w_in, a_ln_g, a_ln_b, a_ws, a_bs, b_mu, b_w0, b_w2, b_a0, b_a2, b_kk, b_ka, b_rk, b_lnx_g,
               b_lnx_b, w_out, ln_g, ln_b):
    split = 3 * D_A + 3 * D_B
    w_main = jnp.concatenate([w_in[:, :split], w_in[:, split + R_W + R_A:]], axis=1).astype(BF16)
    w_tail = w_in[:, split:split + R_W + R_A].astype(BF16)
    zeros = jnp.zeros((R_W, D_B), F32)
    rwkv = (_pad_rows(b_mu, D_B), _row(b_w0),
            jnp.concatenate([b_w2, zeros], 0).astype(BF16), _row(b_a0),
            jnp.concatenate([zeros, b_a2], 0).astype(BF16),
            _row(b_kk), _row(b_ka), _row(b_rk), _row(b_lnx_g), _row(b_lnx_b))
    return dict(w_main=w_main, w_tail=w_tail, a_ln_g=_row(a_ln_g), a_ln_b=_row(a_ln_b), a_ws=a_ws, a_bs=a_bs,
                rwkv=rwkv, w_out=w_out.astype(BF16), ln_g=_row(ln_g), ln_b=_row(ln_b))


def _state_to_quads(wkv):
    b = wkv.shape[0]
    x = wkv.reshape(b, N_QUAD, 4, HEAD_B, HEAD_B).astype(F32)
    eye = jnp.eye(4, dtype=F32)
    return (x[:, :, :, :, None, :] * eye[None, None, :, None, :, None]).reshape(b, N_QUAD, QUAD, QUAD)


def _quads_to_state(st):
    b = st.shape[0]
    x = st.reshape(b, N_QUAD, 4, HEAD_B, 4, HEAD_B)
    x = jnp.diagonal(x, axis1=2, axis2=4)
    return jnp.moveaxis(x, -1, 2).reshape(b, H_B, HEAD_B, HEAD_B)


def _ab_layer(x, shift_prev, wkv0, prm, scan_prec, want_vn):
    bsz, t, _ = x.shape
    m = bsz * t
    x2 = x.reshape(m, D_MODEL)
    tm = min(m, 1024)
    proj, wa = _matmul(x2, prm["w_main"], prm["w_tail"], tm, D_A)

    chunk = min(t, MLP_CHUNK)
    n_chunks = 2 if t >= 2 * MLP_CHUNK else 1
    mask = jnp.tril(jnp.ones((chunk, chunk), dtype=bool))
    wm = jnp.where(mask, prm["a_ws"][:, :chunk, :chunk], 0).astype(BF16)
    bias = jnp.repeat(jnp.swapaxes(prm["a_bs"][:, :chunk], 0, 1), C_A, axis=1).astype(F32)
    out_a, *maybe_vn = _mix_a(proj, prm["a_ln_g"], prm["a_ln_b"], wm, bias, chunk, n_chunks, want_vn)

    sh0 = _pad_rows(shift_prev.astype(F32), D_B)
    out_b, st_new = _rwkv(proj, wa, sh0, _state_to_quads(wkv0), prm["rwkv"], bsz, t, scan_prec)

    w_out = prm["w_out"]
    x_new, x_new_bf = _outproj_ln(out_a, 0, out_b.reshape(m, D_B), 0, x2, w_out[:D_A], w_out[D_A:], prm["ln_g"], prm["ln_b"],
                                  min(m, 512), True)
    last = proj.reshape(bsz, t, -1)[:, -1]
    shift_new = jnp.concatenate([last[:, 3 * D_A:3 * D_A + 3 * D_B], wa.reshape(bsz, t, -1)[:, -1]], axis=-1)
    v_n = maybe_vn[0].reshape(bsz, t, D_A) if want_vn else None
    return (x_new, x_new_bf), shift_new, _quads_to_state(st_new), v_n


def _c_layer(x2, xb, bsz, t, k_cache, v_cache, lam_init, w_in, lam_prm, subg, w_out, ln_g, ln_b):
    m = bsz * t
    tm = min(m, 512)
    wq, wk, wv, wg = (w_in[:, i * D_C:(i + 1) * D_C].astype(BF16) for i in range(4))
    q, = _cproj(xb, wq, "q", tm)
    k_leaf, kb = _cproj(xb, wk, "k", tm)
    v_leaf, vb = _cproj(xb, wv, "v", tm)
    g, = _cproj(xb, wg, "g", tm)
    if k_cache is None:
        o = _attn_prompt(q, kb, vb, g, lam_prm, subg, bsz, t, lam_init, 512)
    else:
        past = k_cache.shape[1]
        o = _attn_sample(q, kb, vb, g, k_cache.reshape(bsz * past, D_C), v_cache.reshape(bsz * past, D_C),
                         lam_prm, subg, bsz, t, past, lam_init)
    x_new, = _outproj_ln(o, 0, o, 1, x2, w_out[:D_C // 2], w_out[D_C // 2:], ln_g, ln_b, min(m, 512), False)
    k_new = k_leaf.reshape(bsz, t, H_C, 2, HEAD_C)
    v_new = v_leaf.reshape(bsz, t, H_C, 2 * HEAD_C)
    return x_new.reshape(bsz, t, D_MODEL), k_new, v_new


def kernel(x_prompt, x_sample, state_b_shift, state_b_wkv, cache_c_k, cache_c_v, ab_w_in, ab_a_ln_g, ab_a_ln_b, ab_a_ws, ab_a_bs, ab_b_mu, ab_b_w0, ab_b_w2, ab_b_a0, ab_b_a2, ab_b_kk, ab_b_ka, ab_b_rk, ab_b_lnx_g, ab_b_lnx_b, ab_w_out, ab_ln_g, ab_ln_b, c_w_in, c_lam_q1, c_lam_k1, c_lam_q2, c_lam_k2, c_subln_g, c_w_out, c_ln_g, c_ln_b):
    scan_prec = None
    bp, tp = x_prompt.shape[:2]
    bs, ts = x_sample.shape[:2]
    x_p, x_s = x_prompt, x_sample
    xp_pair = xs_pair = None
    sh_p_l, wkv_p_l, sh_s_l, wkv_s_l, va_s_l = [], [], [], [], []
    kp_l, vp_l, ks_l, vs_l = [], [], [], []
    for li in range(DEPTH):
        j = li // 2
        if li % 2 == 0:
            prm = _ab_params(ab_w_in[j], ab_a_ln_g[j], ab_a_ln_b[j], ab_a_ws[j], ab_a_bs[j], ab_b_mu[j],
                             ab_b_w0[j], ab_b_w2[j], ab_b_a0[j], ab_b_a2[j], ab_b_kk[j], ab_b_ka[j],
                             ab_b_rk[j], ab_b_lnx_g[j], ab_b_lnx_b[j], ab_w_out[j], ab_ln_g[j], ab_ln_b[j])
            shift0 = jnp.zeros((bp, D_B_SHIFT), F32)
            wkv0 = jnp.zeros((bp, H_B, HEAD_B, HEAD_B), F32)
            xp_pair, sh_p, wkv_p, _ = _ab_layer(x_p, shift0, wkv0, prm, scan_prec, False)
            xs_pair, sh_s, wkv_s, va_s = _ab_layer(x_s, state_b_shift[j], state_b_wkv[j], prm, scan_prec, True)
            sh_p_l.append(sh_p)
            wkv_p_l.append(wkv_p)
            sh_s_l.append(sh_s)
            wkv_s_l.append(wkv_s)
            va_s_l.append(va_s)
        else:
            lam_init = 0.8 - 0.6 * math.exp(-0.3 * li)
            lam_prm = (_row(c_lam_q1[j]), _row(c_lam_k1[j]), _row(c_lam_q2[j]), _row(c_lam_k2[j]))
            args = (lam_init, c_w_in[j], lam_prm, _row(c_subln_g[j]), c_w_out[j].astype(BF16), _row(c_ln_g[j]),
                    _row(c_ln_b[j]))
            x_p, k_p, v_p = _c_layer(*xp_pair, bp, tp, None, None, *args)
            x_s, k_s, v_s = _c_layer(*xs_pair, bs, ts, cache_c_k[j], cache_c_v[j], *args)
            kp_l.append(k_p)
            vp_l.append(v_p)
            ks_l.append(k_s)
            vs_l.append(v_s)
    return (x_p, x_s, jnp.stack(sh_p_l), jnp.stack(wkv_p_l), jnp.stack(sh_s_l), jnp.stack(wkv_s_l),
            jnp.stack(va_s_l), jnp.stack(kp_l), jnp.stack(vp_l), jnp.stack(ks_l), jnp.stack(vs_l))
```

```python
import functools
import math

import jax
import jax.numpy as jnp
from jax import lax
from jax.experimental import pallas as pl
from jax.experimental.pallas import tpu as pltpu

F32 = jnp.float32
BF16 = jnp.bfloat16

D_MODEL = 2048
DEPTH = 2
D_A = 1024
G_A = 4
C_A = D_A // G_A
MLP_CHUNK = 128
D_B = 1024
HEAD_B = 64
H_B = D_B // HEAD_B
R_W = 64
R_A = 64
D_B_SHIFT = 3 * D_B + R_W + R_A
HEAD_C = 128
H_C = 8
D_C = 2048
ATT_CHUNK = 64
DEEPNORM_ALPHA = (2 * DEPTH) ** 0.25
LN_EPS = 1e-5
GN_EPS_B = 64e-5
SUBLN_EPS = 1e-5

QUAD = 4 * HEAD_B
N_QUAD = D_B // QUAD
SCAN_L = 64
VMEM_LIMIT = 56 * 1024 * 1024


def _cparams(sem):
    return pltpu.CompilerParams(dimension_semantics=sem, vmem_limit_bytes=VMEM_LIMIT)


def _dot(a, b, prec=None):
    return jnp.dot(a, b, precision=prec, preferred_element_type=F32)


def _dot_nt(a, b, prec=None):
    return lax.dot_general(a, b, (((1,), (1,)), ((), ())), precision=prec, preferred_element_type=F32)


def _dot_tn(a, b, prec=None):
    return lax.dot_general(a, b, (((0,), (0,)), ((), ())), precision=prec, preferred_element_type=F32)


def _silu(x):
    return x * jax.nn.sigmoid(x)


N_A_COLS = 3


def _abproj_kernel(x_ref, w_ref, wt_ref, lng_ref, lnb_ref, wm_ref, bias_ref, o_ref, ot_ref, oa_ref, *rest, chunk):
    *maybe_vn_ref, xb_ref, u_scr, vn_scr = rest
    j = pl.program_id(1)

    @pl.when(j == 0)
    def _():
        xb = x_ref[...].astype(BF16)
        xb_ref[...] = xb
        ot_ref[...] = _dot(xb, wt_ref[...])

    acc = _dot(xb_ref[...], w_ref[...])

    @pl.when(j == 0)
    def _():
        u_scr[...] = acc

    @pl.when(j == 1)
    def _():
        mu = jnp.mean(acc, -1, keepdims=True)
        vc = acc - mu
        var = jnp.mean(vc * vc, -1, keepdims=True)
        vn = vc * lax.rsqrt(var + LN_EPS) * lng_ref[...] + lnb_ref[...]
        for ref in maybe_vn_ref:
            ref[...] = vn
        vn_scr[...] = vn.astype(BF16)

    @pl.when(j == 2)
    def _():
        for c in range(acc.shape[0] // chunk):
            rows = slice(c * chunk, (c + 1) * chunk)
            parts = [_dot(wm_ref[g], vn_scr[rows, g * C_A:(g + 1) * C_A]) for g in range(G_A)]
            sg = jnp.concatenate(parts, axis=-1) + bias_ref[...]
            oa_ref[rows, :] = u_scr[rows, :] * sg * _silu(acc[rows, :])

    @pl.when(j >= N_A_COLS)
    def _():
        o_ref[...] = acc


def _ab_proj(x, w, w_tail, lng, lnb, wm, bias, tm, chunk, want_vn):
    m, k = x.shape
    n_col = w.shape[1] // D_A
    nt = w_tail.shape[1]
    full = lambda a: pl.BlockSpec(a.shape, lambda i, j: (0,) * a.ndim)
    row = lambda width: pl.BlockSpec((tm, width), lambda i, j: (i, 0))
    n_vn = 1 if want_vn else 0
    return pl.pallas_call(
        functools.partial(_abproj_kernel, chunk=chunk),
        grid=(m // tm, n_col),
        in_specs=[pl.BlockSpec((tm, k), lambda i, j: (i, 0)),
                  pl.BlockSpec((k, D_A), lambda i, j: (0, j)),
                  full(w_tail), full(lng), full(lnb), full(wm), full(bias)],
        out_specs=[pl.BlockSpec((tm, D_A), lambda i, j: (i, jnp.maximum(j - N_A_COLS, 0))),
                   row(nt), row(D_A)] + [row(D_A)] * n_vn,
        out_shape=[jax.ShapeDtypeStruct((m, (n_col - N_A_COLS) * D_A), F32), jax.ShapeDtypeStruct((m, nt), F32),
                   jax.ShapeDtypeStruct((m, D_A), F32)] + [jax.ShapeDtypeStruct((m, D_A), F32)] * n_vn,
        scratch_shapes=[pltpu.VMEM((tm, k), BF16), pltpu.VMEM((tm, D_A), F32), pltpu.VMEM((tm, D_A), BF16)],
        name="ab_proj",
        compiler_params=_cparams(("parallel", "arbitrary")),
    )(x, w, w_tail, lng, lnb, wm, bias)


QK_SCALE = (HEAD_C ** -0.5) * math.log2(math.e)
K_ROWS = 2 * H_C


def _cproj_kernel(x_ref, w_ref, *o_refs, mode):
    acc = _dot(x_ref[...], w_ref[...])
    if mode == "q":
        o_refs[0][...] = (acc * QK_SCALE).astype(BF16)
    elif mode == "k":
        leaf_ref, bf_ref = o_refs
        bf_ref[...] = acc.astype(BF16)
        for c in range(K_ROWS):
            leaf_ref[:, c, :] = acc[:, c * HEAD_C:(c + 1) * HEAD_C]
    elif mode == "v":
        leaf_ref, bf_ref = o_refs
        leaf_ref[...] = acc
        bf_ref[...] = acc.astype(BF16)
    else:
        o_refs[0][...] = acc


def _cproj(xb, w, col, mode, tm):
    m, k = xb.shape
    n = D_C
    flat = lambda dt: (pl.BlockSpec((tm, n), lambda i: (i, 0)), jax.ShapeDtypeStruct((m, n), dt))
    if mode == "q":
        outs = [flat(BF16)]
    elif mode == "k":
        outs = [(pl.BlockSpec((tm, K_ROWS, HEAD_C), lambda i: (i, 0, 0)),
                 jax.ShapeDtypeStruct((m, K_ROWS, HEAD_C), F32)), flat(BF16)]
    elif mode == "v":
        outs = [flat(F32), flat(BF16)]
    else:
        outs = [flat(F32)]
    return pl.pallas_call(
        functools.partial(_cproj_kernel, mode=mode),
        grid=(m // tm,),
        in_specs=[pl.BlockSpec((tm, k), lambda i: (i, 0)), pl.BlockSpec((k, n), lambda i: (0, col))],
        out_specs=[o[0] for o in outs],
        out_shape=[o[1] for o in outs],
        name="cproj_" + mode,
        compiler_params=_cparams(("parallel",)),
    )(xb, w)


def _rwkv_kernel(r_ref, k_ref, v_ref, g_ref, wa_ref, sh0_ref, s0_ref, mu_ref, w0_ref, w2_ref, a0_ref,
                 a2_ref, kkw_ref, kaw_ref, rkw_ref, lng_ref, lnb_ref, ob_ref, sout_ref, prev_scr, st_scr,
                 *, prec):
    L = SCAN_L
    nb = r_ref.shape[0]
    c = pl.program_id(1)

    row4 = lax.broadcasted_iota(jnp.int32, (QUAD, QUAD), 0)
    col4 = lax.broadcasted_iota(jnp.int32, (QUAD, QUAD), 1)
    same_head = (row4 // HEAD_B) == (col4 // HEAD_B)
    ones_head = same_head.astype(F32)

    @pl.when(c == 0)
    def _():
        prev_scr[...] = sh0_ref[...]
        spread = (lax.broadcasted_iota(jnp.int32, (HEAD_B, QUAD), 1) % HEAD_B
                  == lax.broadcasted_iota(jnp.int32, (HEAD_B, QUAD), 0)).astype(F32)
        for b in range(nb):
            for q in range(N_QUAD):
                st_scr[b, q] = _dot(s0_ref[b, q], spread, lax.Precision.HIGHEST) * ones_head
    ones_bf = same_head.astype(BF16)
    fast = prec is None
    mask_bd = ones_bf if fast else ones_head

    def cast(x):
        return x.astype(BF16) if fast else x

    def split2(x):
        hi = x.astype(BF16)
        return jnp.concatenate([hi, (x - hi.astype(F32)).astype(BF16)], axis=0)

    def head_sum(x):
        parts = []
        for q in range(N_QUAD):
            s = _dot(split2(x[:, q * QUAD:(q + 1) * QUAD]), ones_bf)
            parts.append(s[:L] + s[L:])
        return jnp.concatenate(parts, axis=-1)

    tl = lax.broadcasted_iota(jnp.int32, (L, L), 0)
    sl = lax.broadcasted_iota(jnp.int32, (L, L), 1)
    tri_incl = (sl <= tl).astype(F32)

    def prepare(b):
        def shifted(ref, idx, width):
            x = ref[b]
            first = lax.broadcasted_iota(jnp.int32, (L, width), 0) == 0
            prev = jnp.where(first, prev_scr[b, idx:idx + 1, :width], pltpu.roll(x, 1, 0))
            prev_scr[b, idx:idx + 1, :width] = x[L - 1:L, :]
            return x + (prev - x) * mu_ref[idx:idx + 1, :width]

        r = shifted(r_ref, 0, D_B)
        k = shifted(k_ref, 1, D_B)
        v = shifted(v_ref, 2, D_B)
        wa = shifted(wa_ref, 3, R_W + R_A)

        z = w0_ref[...] + _dot(jnp.tanh(wa).astype(BF16), w2_ref[...])
        nz = -z
        softplus = jnp.maximum(nz, 0.0) + jnp.log(1.0 + jnp.exp(-jnp.abs(nz)))
        log_decay = -jnp.exp(-softplus - 0.5)
        a = jax.nn.sigmoid(a0_ref[...] + _dot(wa.astype(BF16), a2_ref[...]))

        kk = k * kkw_ref[...]
        kk = kk / jnp.maximum(jnp.sqrt(head_sum(kk * kk)), 1e-12)
        k_mod = k * (1.0 + (a - 1.0) * kaw_ref[...])
        bonus = head_sum(r * k_mod * rkw_ref[...]) * v

        cum = _dot(tri_incl, log_decay, lax.Precision.HIGHEST)
        cum_last = cum[L - 1:L, :]
        p_inv = jnp.exp(-cum)
        p_tail = jnp.exp(cum_last - cum)
        beta = kk * a
        return dict(
            kap=cast(kk * jnp.exp(cum - log_decay)), bet=cast(beta * p_inv), kt=cast(k_mod * p_inv),
            rt=cast(r * jnp.exp(cum)), kt_tail=cast(k_mod * p_tail), bet_tail=cast(-(beta * p_tail)),
            vc=cast(v), p_last=jnp.exp(cum_last), bonus=bonus)

    tq = lax.broadcasted_iota(jnp.int32, (L, QUAD), 0)
    sq = lax.broadcasted_iota(jnp.int32, (L, QUAD), 1) % HEAD_B
    strict = sq < tq
    incl = sq <= tq
    eye_c = (sq == tq).astype(F32)
    blk16 = (tq // 16) == (sq // 16)
    blk32 = (tq // 32) == (sq // 32)
    strict16 = strict & blk16
    off32 = strict & blk32 & jnp.logical_not(blk16)
    off64 = strict & jnp.logical_not(blk32)

    def bd(x):
        return jnp.concatenate([x] * (QUAD // L), axis=0) * mask_bd

    def mm(x, y_bd):
        return _dot(cast(x), y_bd, prec)

    pre = [prepare(b) for b in range(nb)]

    chains = [(b, q) for b in range(nb) for q in range(N_QUAD)]
    n = range(len(chains))
    lanes = [slice(q * QUAD, (q + 1) * QUAD) for _, q in chains]
    sub = lambda name, i: pre[chains[i][0]][name][:, lanes[i]]
    st = [st_scr[b, q] for b, q in chains]
    kr = [jnp.concatenate([sub("kap", i), sub("rt", i)], axis=0) for i in n]
    g_b = [_dot_nt(kr[i], bd(sub("bet", i)), prec) for i in n]
    g_k = [_dot_nt(kr[i], bd(sub("kt", i)), prec) for i in n]
    g_s = [_dot_nt(kr[i], cast(st[i]), prec) for i in n]
    bd_v = [bd(sub("vc", i)) for i in n]

    pw = [jnp.where(strict16, -g_b[i][:L], 0.0) for i in n]
    t_inv = [eye_c + pw[i] for i in n]
    for _ in range(3):
        pw = [mm(pw[i], bd(cast(pw[i]))) for i in n]
        t_inv = [t_inv[i] + mm(t_inv[i], bd(cast(pw[i]))) for i in n]
    rhs = [g_s[i][:L] + mm(jnp.where(strict, g_k[i][:L], 0.0), bd_v[i]) for i in n]
    for off_mask in (off32, off64):
        low = [mm(jnp.where(off_mask, g_b[i][:L], 0.0), bd(cast(t_inv[i]))) for i in n]
        t_inv = [t_inv[i] - mm(t_inv[i], bd(cast(low[i]))) for i in n]

    u_c = [cast(mm(t_inv[i], bd(cast(rhs[i])))) for i in n]
    ys = [g_s[i][L:] + mm(jnp.where(incl, g_k[i][L:], 0.0), bd_v[i])
          - mm(jnp.where(incl, g_b[i][L:], 0.0), bd(u_c[i])) for i in n]
    for i in n:
        b, q = chains[i]
        upd = _dot_tn(jnp.concatenate([sub("vc", i), u_c[i]], axis=0),
                      jnp.concatenate([sub("kt_tail", i), sub("bet_tail", i)], axis=0), prec)
        st_scr[b, q] = (st[i] * sub("p_last", i) + upd) * ones_head

    @pl.when(c == pl.num_programs(1) - 1)
    def _():
        gather = (lax.broadcasted_iota(jnp.int32, (QUAD, HEAD_B), 0) % HEAD_B
                  == lax.broadcasted_iota(jnp.int32, (QUAD, HEAD_B), 1)).astype(F32)
        for b in range(nb):
            for q in range(N_QUAD):
                sout_ref[b, q] = _dot(st_scr[b, q], gather, lax.Precision.HIGHEST)

    for b in range(nb):
        y = jnp.concatenate(ys[b * N_QUAD:(b + 1) * N_QUAD], axis=-1)
        mu_y = head_sum(y) * (1.0 / HEAD_B)
        yc = y - mu_y
        var_y = head_sum(yc * yc) * (1.0 / HEAD_B)
        yn = yc * lax.rsqrt(var_y + GN_EPS_B) * lng_ref[...] + lnb_ref[...]
        ob_ref[b] = (yn + pre[b]["bonus"]) * _silu(g_ref[b])


SCAN_NB = 4


def _rwkv(proj, tail, sh0, s0, prm, bsz, t, prec):
    nb = SCAN_NB
    nc = t // SCAN_L
    proj3 = proj.reshape(bsz, t, -1)
    tail3 = tail.reshape(bsz, t, -1)
    col = lambda j: pl.BlockSpec((nb, SCAN_L, D_B), lambda b, c, j=j: (b, c, j))
    full = lambda a: pl.BlockSpec(a.shape, lambda b, c: (0,) * a.ndim)
    per_b = lambda a: pl.BlockSpec((nb,) + a.shape[1:], lambda b, c: (b,) + (0,) * (a.ndim - 1))
    return pl.pallas_call(
        functools.partial(_rwkv_kernel, prec=prec),
        grid=(bsz // nb, nc),
        in_specs=[col(0), col(1), col(2), col(0),
                  pl.BlockSpec((nb, SCAN_L, R_W + R_A), lambda b, c: (b, c, D_B // (R_W + R_A))),
                  per_b(sh0), per_b(s0)] + [full(p) for p in prm],
        out_specs=[pl.BlockSpec((nb, SCAN_L, D_B), lambda b, c: (b, c, 0)), per_b(s0)],
        out_shape=[jax.ShapeDtypeStruct((bsz, t, D_B), F32), jax.ShapeDtypeStruct(s0.shape, F32)],
        scratch_shapes=[pltpu.VMEM((nb, 4, D_B), F32), pltpu.VMEM((nb, N_QUAD, QUAD, QUAD), F32)],
        name="rwkv_scan",
        compiler_params=_cparams(("parallel", "arbitrary")),
    )(proj3, proj3, proj3, tail3, tail3, sh0, s0, *prm)


def _outproj_ln_kernel(a_ref, b_ref, x_ref, w1_ref, w2_ref, g_ref, bias_ref, o_ref, *maybe_bf_ref):
    acc = _dot(a_ref[...].astype(BF16), w1_ref[...]) + _dot(b_ref[...].astype(BF16), w2_ref[...])
    y = DEEPNORM_ALPHA * x_ref[...] + acc
    mu = jnp.mean(y, -1, keepdims=True)
    yc = y - mu
    var = jnp.mean(yc * yc, -1, keepdims=True)
    out = yc * lax.rsqrt(var + LN_EPS) * g_ref[...] + bias_ref[...]
    o_ref[...] = out
    for ref in maybe_bf_ref:
        ref[...] = out.astype(BF16)


def _outproj_ln(a, a_col, b, b_col, x, w, g, bias, tm, want_bf16):
    m = x.shape[0]
    half = D_MODEL // 2
    full = lambda arr: pl.BlockSpec(arr.shape, lambda i: (0,) * arr.ndim)
    row = pl.BlockSpec((tm, D_MODEL), lambda i: (i, 0))
    n_out = 2 if want_bf16 else 1
    return pl.pallas_call(
        _outproj_ln_kernel,
        grid=(m // tm,),
        in_specs=[pl.BlockSpec((tm, half), lambda i: (i, a_col)),
                  pl.BlockSpec((tm, half), lambda i: (i, b_col)),
                  row, pl.BlockSpec((half, D_MODEL), lambda i: (0, 0)),
                  pl.BlockSpec((half, D_MODEL), lambda i: (1, 0)), full(g), full(bias)],
        out_specs=[row] * n_out,
        out_shape=[jax.ShapeDtypeStruct((m, D_MODEL), F32), jax.ShapeDtypeStruct((m, D_MODEL), BF16)][:n_out],
        name="outproj_ln",
        compiler_params=_cparams(("parallel",)),
    )(a, b, x, w, w, g, bias)


def _lambda(lq1_ref, lk1_ref, lq2_ref, lk2_ref, lam_init):
    s1 = jnp.sum(lq1_ref[...] * lk1_ref[...], -1, keepdims=True)
    s2 = jnp.sum(lq2_ref[...] * lk2_ref[...], -1, keepdims=True)
    return jnp.exp(s1) - jnp.exp(s2) + lam_init


def _attn_finish(o, g, subg_ref, lam_init):
    o = o * lax.rsqrt(jnp.mean(o * o, -1, keepdims=True) + SUBLN_EPS) * subg_ref[...]
    return o * (1.0 - lam_init) * _silu(g)


ATT_ROWS = 128
LANES = 128


def _attn_prompt_kernel(q_ref, k_ref, v_ref, g_ref, lq1_ref, lk1_ref, lq2_ref, lk2_ref, subg_ref, o_ref,
                        m0_ref, m1_ref, l0_ref, l1_ref, acc0_ref, acc1_ref, sa_ref, sb_ref, *, tq, lam_init):
    qi = pl.program_id(2)
    lam = _lambda(lq1_ref, lk1_ref, lq2_ref, lk2_ref, lam_init)
    ms = (m0_ref, m1_ref)
    ls = (l0_ref, l1_ref)
    accs = (acc0_ref, acc1_ref)
    n_sub = tq // ATT_ROWS
    n_rep = tq // LANES
    for j in range(2):
        ms[j][...] = jnp.full((tq, LANES), -jnp.inf, F32)
        ls[j][...] = jnp.zeros((tq, LANES), F32)
        accs[j][...] = jnp.zeros((tq, 2 * HEAD_C), F32)

    chains = [(r, j) for r in range(n_sub) for j in range(2)]

    def blk(i):
        return pl.multiple_of(i * tq, tq)

    def put_scores(r, j, off, s_ref):
        rows = pl.ds(r * ATT_ROWS, ATT_ROWS)
        cols = slice(j * HEAD_C, (j + 1) * HEAD_C)
        s_ref[j, rows, :] = _dot_nt(q_ref[rows, cols], k_ref[pl.ds(off, tq), cols])

    def consume(r, j, s_ref, off, masked):
        rows = pl.ds(r * ATT_ROWS, ATT_ROWS)
        s = s_ref[j, rows, :]
        if masked:
            rq = (lax.broadcasted_iota(jnp.int32, (ATT_ROWS, tq), 0) + r * ATT_ROWS) // ATT_CHUNK
            ck = lax.broadcasted_iota(jnp.int32, (ATT_ROWS, tq), 1) // ATT_CHUNK
            s = jnp.where(ck <= rq, s, -jnp.inf)
        m_old = ms[j][rows, :]
        m_new = jnp.maximum(m_old, jnp.max(s, -1, keepdims=True))
        alpha = jnp.exp2(m_old - m_new)
        p = jnp.exp2(s - jnp.concatenate([m_new] * n_rep, axis=1))
        part = p[:, :LANES]
        for c in range(1, n_rep):
            part = part + p[:, c * LANES:(c + 1) * LANES]
        ls[j][rows, :] = alpha * ls[j][rows, :] + part
        accs[j][rows, :] = (jnp.concatenate([alpha] * 2, axis=1) * accs[j][rows, :]
                            + _dot(p.astype(BF16), v_ref[pl.ds(off, tq), :]))
        ms[j][rows, :] = m_new

    def step(s_cur, off_cur, masked, s_next=None, off_next=None):
        for r, j in chains:
            if s_next is not None:
                put_scores(r, j, off_next, s_next)
            consume(r, j, s_cur, off_cur, masked)

    for r, j in chains:
        put_scores(r, j, blk(0), sa_ref)

    def pair(p, carry):
        step(sa_ref, blk(2 * p), False, sb_ref, blk(2 * p + 1))
        step(sb_ref, blk(2 * p + 1), False, sa_ref, blk(2 * p + 2))
        return carry

    lax.fori_loop(0, qi // 2, pair, 0)

    @pl.when(qi % 2 == 1)
    def _():
        step(sa_ref, blk(qi - 1), False, sb_ref, blk(qi))
        step(sb_ref, blk(qi), True)

    @pl.when(qi % 2 == 0)
    def _():
        step(sa_ref, blk(qi), True)

    l0 = jnp.sum(l0_ref[...], -1, keepdims=True)
    l1 = jnp.sum(l1_ref[...], -1, keepdims=True)
    o = acc0_ref[...] / l0 - lam * (acc1_ref[...] / l1)
    o_ref[...] = _attn_finish(o, g_ref[...], subg_ref, lam_init)


def _attn_prompt(q, k, v, g, lam_prm, subg, bsz, t, lam_init, tq):
    nq = t // tq
    hw = 2 * HEAD_C
    full = lambda a: pl.BlockSpec(a.shape, lambda b, h, i: (0,) * a.ndim)
    tile = pl.BlockSpec((tq, hw), lambda b, h, i: (b * nq + i, h))
    whole = pl.BlockSpec((t, hw), lambda b, h, i: (b, h))
    return pl.pallas_call(
        functools.partial(_attn_prompt_kernel, tq=tq, lam_init=lam_init),
        grid=(bsz, H_C, nq),
        in_specs=[tile, whole, whole, tile] + [full(p) for p in lam_prm] + [full(subg)],
        out_specs=tile,
        out_shape=jax.ShapeDtypeStruct((bsz * t, D_C), F32),
        scratch_shapes=([pltpu.VMEM((tq, LANES), F32)] * 4 + [pltpu.VMEM((tq, hw), F32)] * 2
                        + [pltpu.VMEM((2, tq, tq), F32)] * 2),
        name="attn_prompt",
        compiler_params=_cparams(("parallel", "parallel", "arbitrary")),
    )(q, k, v, g, *lam_prm, subg)


def _attn_sample_kernel(q_ref, kn_ref, vn_ref, g_ref, kc_hbm, vc_hbm, lq1_ref, lk1_ref, lq2_ref,
                        lk2_ref, subg_ref, o_ref, kbuf, vbuf, sem, *, lam_init, past):
    step = pl.program_id(0)
    slot = step % 2

    def cache_copies(s, into):
        b = s // H_C
        h = s % H_C
        rows = pl.ds(b * past, past)
        return (pltpu.make_async_copy(kc_hbm.at[rows, 2 * h], kbuf.at[into, 0], sem.at[into, 0]),
                pltpu.make_async_copy(kc_hbm.at[rows, 2 * h + 1], kbuf.at[into, 1], sem.at[into, 1]),
                pltpu.make_async_copy(vc_hbm.at[rows, h], vbuf.at[into], sem.at[into, 2]))

    @pl.when(step == 0)
    def _():
        for cp in cache_copies(0, 0):
            cp.start()

    @pl.when(step + 1 < pl.num_programs(0))
    def _():
        for cp in cache_copies(step + 1, 1 - slot):
            cp.start()

    for cp in cache_copies(step, slot):
        cp.wait()

    lam = _lambda(lq1_ref, lk1_ref, lq2_ref, lk2_ref, lam_init)
    q = q_ref[...]
    kn = kn_ref[...]
    probs = []
    for j in range(2):
        cols = slice(j * HEAD_C, (j + 1) * HEAD_C)
        sc = _dot_nt(q[:, cols], kbuf[slot, j].astype(BF16))
        sn = _dot_nt(q[:, cols], kn[:, cols])
        m = jnp.maximum(jnp.max(sc, -1, keepdims=True), jnp.max(sn, -1, keepdims=True))
        pc = jnp.exp2(sc - m)
        pn = jnp.exp2(sn - m)
        inv = 1.0 / (jnp.sum(pc, -1, keepdims=True) + jnp.sum(pn, -1, keepdims=True))
        probs.append((pc * inv, pn * inv))
    ac = probs[0][0] - lam * probs[1][0]
    an = probs[0][1] - lam * probs[1][1]
    o = _dot(ac.astype(BF16), vbuf[slot].astype(BF16)) + _dot(an.astype(BF16), vn_ref[...])
    o_ref[...] = _attn_finish(o, g_ref[...], subg_ref, lam_init)


def _attn_sample(q, kn, vn, g, kc, vc, lam_prm, subg, bsz, t, past, lam_init):
    hw = 2 * HEAD_C
    full = lambda a: pl.BlockSpec(a.shape, lambda s: (0,) * a.ndim)
    new = pl.BlockSpec((t, hw), lambda s: (s // H_C, s % H_C))
    hbm = pl.BlockSpec(memory_space=pl.ANY)
    return pl.pallas_call(
        functools.partial(_attn_sample_kernel, lam_init=lam_init, past=past),
        grid=(bsz * H_C,),
        in_specs=[new, new, new, new, hbm, hbm] + [full(p) for p in lam_prm] + [full(subg)],
        out_specs=new,
        out_shape=jax.ShapeDtypeStruct((bsz * t, D_C), F32),
        scratch_shapes=[pltpu.VMEM((2, 2, past, HEAD_C), F32), pltpu.VMEM((2, past, hw), F32),
                        pltpu.SemaphoreType.DMA((2, 3))],
        name="attn_sample",
        compiler_params=_cparams(("arbitrary",)),
    )(q, kn, vn, g, kc, vc, *lam_prm, subg)


def _row(p):
    return p.reshape(1, -1).astype(F32)


def _pad_rows(x, width):
    lead = x.shape[:-1]
    main = x[..., :3 * width].reshape(lead + (3, width))
    rest = jnp.pad(x[..., 3 * width:], [(0, 0)] * len(lead) + [(0, 4 * width - x.shape[-1])])
    return jnp.concatenate([main, rest[..., None, :]], axis=-2)


def _ab_params(w_in, a_ln_g, a_ln_b, a_ws, a_bs, b_mu, b_w0, b_w2, b_a0, b_a2, b_kk, b_ka, b_rk, b_lnx_g,
               b_lnx_b, w_out, ln_g, ln_b):
    split = 3 * D_A + 3 * D_B
    w_main = w_in[:, :split].astype(BF16)
    w_tail = jnp.concatenate([w_in[:, split + R_W + R_A:], w_in[:, split:split + R_W + R_A]], axis=1).astype(BF16)
    zeros = jnp.zeros((R_W, D_B), F32)
    rwkv = (_pad_rows(b_mu, D_B), _row(b_w0),
            jnp.concatenate([b_w2, zeros], 0).astype(BF16), _row(b_a0),
            jnp.concatenate([zeros, b_a2], 0).astype(BF16),
            _row(b_kk), _row(b_ka), _row(b_rk), _row(b_lnx_g), _row(b_lnx_b))
    return dict(w_main=w_main, w_tail=w_tail, a_ln_g=_row(a_ln_g), a_ln_b=_row(a_ln_b), a_ws=a_ws, a_bs=a_bs,
                rwkv=rwkv, w_out=w_out.astype(BF16), ln_g=_row(ln_g), ln_b=_row(ln_b))


def _ab_layer(x, shift_prev, wkv0, prm, scan_prec, want_vn):
    bsz, t, _ = x.shape
    m = bsz * t
    x2 = x.reshape(m, D_MODEL)
    chunk = min(t, MLP_CHUNK)
    mask = jnp.tril(jnp.ones((chunk, chunk), dtype=bool))
    wm = jnp.where(mask, prm["a_ws"][:, :chunk, :chunk], 0).astype(BF16)
    bias = jnp.repeat(jnp.swapaxes(prm["a_bs"][:, :chunk], 0, 1), C_A, axis=1).astype(F32)
    proj, tail, out_a, *maybe_vn = _ab_proj(x2, prm["w_main"], prm["w_tail"], prm["a_ln_g"], prm["a_ln_b"], wm, bias,
                                            min(m, 512), chunk, want_vn)

    sh0 = _pad_rows(shift_prev.astype(F32), D_B)
    s0 = wkv0.astype(F32).reshape(bsz, N_QUAD, QUAD, HEAD_B)
    out_b, st_new = _rwkv(proj, tail, sh0, s0, prm["rwkv"], bsz, t, scan_prec)

    x_new, x_new_bf = _outproj_ln(out_a, 0, out_b.reshape(m, D_B), 0, x2, prm["w_out"], prm["ln_g"], prm["ln_b"],
                                  min(m, 512), True)
    last = proj.reshape(bsz, t, -1)[:, -1]
    shift_new = jnp.concatenate([last, tail.reshape(bsz, t, -1)[:, -1, D_B:]], axis=-1)
    v_n = maybe_vn[0].reshape(bsz, t, D_A) if want_vn else None
    return (x_new, x_new_bf), shift_new, st_new.reshape(bsz, H_B, HEAD_B, HEAD_B), v_n


def _c_layer(x2, xb, bsz, t, k_cache, v_cache, lam_init, w_in, lam_prm, subg, w_out, ln_g, ln_b):
    m = bsz * t
    tm = min(m, 512)
    q, = _cproj(xb, w_in, 0, "q", tm)
    k_leaf, kb = _cproj(xb, w_in, 1, "k", tm)
    v_leaf, vb = _cproj(xb, w_in, 2, "v", tm)
    g, = _cproj(xb, w_in, 3, "g", tm)
    if k_cache is None:
        o = _attn_prompt(q, kb, vb, g, lam_prm, subg, bsz, t, lam_init, 512)
    else:
        past = k_cache.shape[1]
        o = _attn_sample(q, kb, vb, g, k_cache.reshape(bsz * past, K_ROWS, HEAD_C),
                         v_cache.reshape(bsz * past, H_C, 2 * HEAD_C), lam_prm, subg, bsz, t, past, lam_init)
    x_new, = _outproj_ln(o, 0, o, 1, x2, w_out, ln_g, ln_b, min(m, 512), False)
    k_new = k_leaf.reshape(bsz, t, H_C, 2, HEAD_C)
    v_new = v_leaf.reshape(bsz, t, H_C, 2 * HEAD_C)
    return x_new.reshape(bsz, t, D_MODEL), k_new, v_new


def kernel(x_prompt, x_sample, state_b_shift, state_b_wkv, cache_c_k, cache_c_v, ab_w_in, ab_a_ln_g, ab_a_ln_b, ab_a_ws, ab_a_bs, ab_b_mu, ab_b_w0, ab_b_w2, ab_b_a0, ab_b_a2, ab_b_kk, ab_b_ka, ab_b_rk, ab_b_lnx_g, ab_b_lnx_b, ab_w_out, ab_ln_g, ab_ln_b, c_w_in, c_lam_q1, c_lam_k1, c_lam_q2, c_lam_k2, c_subln_g, c_w_out, c_ln_g, c_ln_b):
    scan_prec = None
    bp, tp = x_prompt.shape[:2]
    bs, ts = x_sample.shape[:2]
    x_p, x_s = x_prompt, x_sample
    xp_pair = xs_pair = None
    sh_p_l, wkv_p_l, sh_s_l, wkv_s_l, va_s_l = [], [], [], [], []
    kp_l, vp_l, ks_l, vs_l = [], [], [], []
    for li in range(DEPTH):
        j = li // 2
        if li % 2 == 0:
            prm = _ab_params(ab_w_in[j], ab_a_ln_g[j], ab_a_ln_b[j], ab_a_ws[j], ab_a_bs[j], ab_b_mu[j],
                             ab_b_w0[j], ab_b_w2[j], ab_b_a0[j], ab_b_a2[j], ab_b_kk[j], ab_b_ka[j],
                             ab_b_rk[j], ab_b_lnx_g[j], ab_b_lnx_b[j], ab_w_out[j], ab_ln_g[j], ab_ln_b[j])
            shift0 = jnp.zeros((bp, D_B_SHIFT), F32)
            wkv0 = jnp.zeros((bp, H_B, HEAD_B, HEAD_B), F32)
            xp_pair, sh_p, wkv_p, _ = _ab_layer(x_p, shift0, wkv0, prm, scan_prec, False)
            xs_pair, sh_s, wkv_s, va_s = _ab_layer(x_s, state_b_shift[j], state_b_wkv[j], prm, scan_prec, True)
            sh_p_l.append(sh_p)
            wkv_p_l.append(wkv_p)
            sh_s_l.append(sh_s)
            wkv_s_l.append(wkv_s)
            va_s_l.append(va_s)
        else:
            lam_init = 0.8 - 0.6 * math.exp(-0.3 * li)
            lam_prm = (_row(c_lam_q1[j]), _row(c_lam_k1[j]), _row(c_lam_q2[j]), _row(c_lam_k2[j]))
            args = (lam_init, c_w_in[j].astype(BF16), lam_prm, _row(c_subln_g[j]), c_w_out[j].astype(BF16), _row(c_ln_g[j]),
                    _row(c_ln_b[j]))
            x_p, k_p, v_p = _c_layer(*xp_pair, bp, tp, None, None, *args)
            x_s, k_s, v_s = _c_layer(*xs_pair, bs, ts, cache_c_k[j], cache_c_v[j], *args)
            kp_l.append(k_p)
            vp_l.append(v_p)
            ks_l.append(k_s)
            vs_l.append(v_s)
    return (x_p, x_s, jnp.stack(sh_p_l), jnp.stack(wkv_p_l), jnp.stack(sh_s_l), jnp.stack(wkv_s_l),
            jnp.stack(va_s_l), jnp.stack(kp_l), jnp.stack(vp_l), jnp.stack(ks_l), jnp.stack(vs_l))
```

```python
import functools
import math

import jax
import jax.numpy as jnp
from jax import lax
from jax.experimental import pallas as pl
from jax.experimental.pallas import tpu as pltpu

F32 = jnp.float32
BF16 = jnp.bfloat16

D_MODEL = 2048
DEPTH = 2
D_A = 1024
G_A = 4
C_A = D_A // G_A
MLP_CHUNK = 128
D_B = 1024
HEAD_B = 64
H_B = D_B // HEAD_B
R_W = 64
R_A = 64
D_B_SHIFT = 3 * D_B + R_W + R_A
HEAD_C = 128
H_C = 8
D_C = 2048
ATT_CHUNK = 64
DEEPNORM_ALPHA = (2 * DEPTH) ** 0.25
LN_EPS = 1e-5
GN_EPS_B = 64e-5
SUBLN_EPS = 1e-5

QUAD = 4 * HEAD_B
N_QUAD = D_B // QUAD
SCAN_L = 64
VMEM_LIMIT = 56 * 1024 * 1024


def _cparams(sem):
    return pltpu.CompilerParams(dimension_semantics=sem, vmem_limit_bytes=VMEM_LIMIT)


def _dot(a, b, prec=None):
    return jnp.dot(a, b, precision=prec, preferred_element_type=F32)


def _dot_nt(a, b, prec=None):
    return lax.dot_general(a, b, (((1,), (1,)), ((), ())), precision=prec, preferred_element_type=F32)


def _dot_tn(a, b, prec=None):
    return lax.dot_general(a, b, (((0,), (0,)), ((), ())), precision=prec, preferred_element_type=F32)


def _silu(x):
    return x * jax.nn.sigmoid(x)


N_A_COLS = 3


def _abproj_kernel(x_ref, w_ref, wt_ref, lng_ref, lnb_ref, wm_ref, bias_ref, o_ref, ot_ref, oa_ref, *rest, chunk):
    *maybe_vn_ref, xb_ref, u_scr, vn_scr = rest
    j = pl.program_id(1)

    @pl.when(j == 0)
    def _():
        xb = x_ref[...].astype(BF16)
        xb_ref[...] = xb
        ot_ref[...] = _dot(xb, wt_ref[...])

    acc = _dot(xb_ref[...], w_ref[...])

    @pl.when(j == 0)
    def _():
        u_scr[...] = acc

    @pl.when(j == 1)
    def _():
        mu = jnp.mean(acc, -1, keepdims=True)
        vc = acc - mu
        var = jnp.mean(vc * vc, -1, keepdims=True)
        vn = vc * lax.rsqrt(var + LN_EPS) * lng_ref[...] + lnb_ref[...]
        for ref in maybe_vn_ref:
            ref[...] = vn
        vn_scr[...] = vn.astype(BF16)

    @pl.when(j == 2)
    def _():
        for c in range(acc.shape[0] // chunk):
            rows = slice(c * chunk, (c + 1) * chunk)
            parts = [_dot(wm_ref[g], vn_scr[rows, g * C_A:(g + 1) * C_A]) for g in range(G_A)]
            sg = jnp.concatenate(parts, axis=-1) + bias_ref[...]
            oa_ref[rows, :] = u_scr[rows, :] * sg * _silu(acc[rows, :])

    @pl.when(j >= N_A_COLS)
    def _():
        o_ref[...] = acc


def _ab_proj(x, w, w_tail, lng, lnb, wm, bias, tm, chunk, want_vn):
    m, k = x.shape
    n_col = w.shape[1] // D_A
    nt = w_tail.shape[1]
    full = lambda a: pl.BlockSpec(a.shape, lambda i, j: (0,) * a.ndim)
    row = lambda width: pl.BlockSpec((tm, width), lambda i, j: (i, 0))
    n_vn = 1 if want_vn else 0
    return pl.pallas_call(
        functools.partial(_abproj_kernel, chunk=chunk),
        grid=(m // tm, n_col),
        in_specs=[pl.BlockSpec((tm, k), lambda i, j: (i, 0)),
                  pl.BlockSpec((k, D_A), lambda i, j: (0, j)),
                  full(w_tail), full(lng), full(lnb), full(wm), full(bias)],
        out_specs=[pl.BlockSpec((tm, D_A), lambda i, j: (i, jnp.maximum(j - N_A_COLS, 0))),
                   row(nt), row(D_A)] + [row(D_A)] * n_vn,
        out_shape=[jax.ShapeDtypeStruct((m, (n_col - N_A_COLS) * D_A), F32), jax.ShapeDtypeStruct((m, nt), F32),
                   jax.ShapeDtypeStruct((m, D_A), F32)] + [jax.ShapeDtypeStruct((m, D_A), F32)] * n_vn,
        scratch_shapes=[pltpu.VMEM((tm, k), BF16), pltpu.VMEM((tm, D_A), F32), pltpu.VMEM((tm, D_A), BF16)],
        name="ab_proj",
        compiler_params=_cparams(("parallel", "arbitrary")),
    )(x, w, w_tail, lng, lnb, wm, bias)


QK_SCALE = (HEAD_C ** -0.5) * math.log2(math.e)
K_ROWS = 2 * H_C


def _cproj_kernel(x_ref, w_ref, *o_refs, mode):
    acc = _dot(x_ref[...], w_ref[...])
    if mode == "q":
        o_refs[0][...] = (acc * QK_SCALE).astype(BF16)
    elif mode == "k":
        leaf_hbm, bf_ref, stage, sem = o_refs
        i = pl.program_id(0)
        n_steps = pl.num_programs(0)
        tm = acc.shape[0]
        slot = i % 2

        def leaf_copies(s, src):
            rows = pl.ds(s * tm, tm)
            return [pltpu.make_async_copy(stage.at[src, :, pl.ds(c * HEAD_C, HEAD_C)], leaf_hbm.at[rows, c],
                                          sem.at[src]) for c in range(K_ROWS)]

        bf_ref[...] = acc.astype(BF16)

        @pl.when(i >= 2)
        def _():
            for cp in leaf_copies(i - 2, slot):
                cp.wait()

        stage[slot] = acc
        for cp in leaf_copies(i, slot):
            cp.start()

        @pl.when(i == n_steps - 1)
        def _():
            @pl.when(i >= 1)
            def _():
                for cp in leaf_copies(i - 1, 1 - slot):
                    cp.wait()

            for cp in leaf_copies(i, slot):
                cp.wait()
    elif mode == "v":
        leaf_ref, bf_ref = o_refs
        leaf_ref[...] = acc
        bf_ref[...] = acc.astype(BF16)
    else:
        o_refs[0][...] = acc


def _cproj(xb, w, col, mode, tm):
    m, k = xb.shape
    n = D_C
    flat = lambda dt: (pl.BlockSpec((tm, n), lambda i: (i, 0)), jax.ShapeDtypeStruct((m, n), dt))
    if mode == "q":
        outs = [flat(BF16)]
    elif mode == "k":
        outs = [(pl.BlockSpec(memory_space=pl.ANY), jax.ShapeDtypeStruct((m, K_ROWS, HEAD_C), F32)), flat(BF16)]
    elif mode == "v":
        outs = [flat(F32), flat(BF16)]
    else:
        outs = [flat(F32)]
    scratch = [pltpu.VMEM((2, tm, n), F32), pltpu.SemaphoreType.DMA((2,))] if mode == "k" else []
    return pl.pallas_call(
        functools.partial(_cproj_kernel, mode=mode),
        grid=(m // tm,),
        in_specs=[pl.BlockSpec((tm, k), lambda i: (i, 0)), pl.BlockSpec((k, n), lambda i: (0, col))],
        out_specs=[o[0] for o in outs],
        out_shape=[o[1] for o in outs],
        scratch_shapes=scratch,
        name="cproj_" + mode,
        compiler_params=_cparams(("arbitrary" if mode == "k" else "parallel",)),
    )(xb, w)


def _rwkv_kernel(r_ref, k_ref, v_ref, g_ref, wa_ref, sh0_ref, s0_ref, mu_ref, w0_ref, w2_ref, a0_ref,
                 a2_ref, kkw_ref, kaw_ref, rkw_ref, lng_ref, lnb_ref, ob_ref, sout_ref, prev_scr, st_scr,
                 *, prec):
    L = SCAN_L
    nb = r_ref.shape[0]
    c = pl.program_id(1)

    row4 = lax.broadcasted_iota(jnp.int32, (QUAD, QUAD), 0)
    col4 = lax.broadcasted_iota(jnp.int32, (QUAD, QUAD), 1)
    same_head = (row4 // HEAD_B) == (col4 // HEAD_B)
    ones_head = same_head.astype(F32)

    @pl.when(c == 0)
    def _():
        prev_scr[...] = sh0_ref[...]
        spread = (lax.broadcasted_iota(jnp.int32, (HEAD_B, QUAD), 1) % HEAD_B
                  == lax.broadcasted_iota(jnp.int32, (HEAD_B, QUAD), 0)).astype(F32)
        for b in range(nb):
            for q in range(N_QUAD):
                st_scr[b, q] = _dot(s0_ref[b, q], spread, lax.Precision.HIGHEST) * ones_head
    ones_bf = same_head.astype(BF16)
    fast = prec is None
    mask_bd = ones_bf if fast else ones_head

    def cast(x):
        return x.astype(BF16) if fast else x

    def split2(x):
        hi = x.astype(BF16)
        return jnp.concatenate([hi, (x - hi.astype(F32)).astype(BF16)], axis=0)

    def head_sum(x):
        parts = []
        for q in range(N_QUAD):
            s = _dot(split2(x[:, q * QUAD:(q + 1) * QUAD]), ones_bf)
            parts.append(s[:L] + s[L:])
        return jnp.concatenate(parts, axis=-1)

    tl = lax.broadcasted_iota(jnp.int32, (L, L), 0)
    sl = lax.broadcasted_iota(jnp.int32, (L, L), 1)
    tri_incl = (sl <= tl).astype(F32)

    def prepare(b):
        def shifted(ref, idx, width):
            x = ref[b]
            first = lax.broadcasted_iota(jnp.int32, (L, width), 0) == 0
            prev = jnp.where(first, prev_scr[b, idx:idx + 1, :width], pltpu.roll(x, 1, 0))
            prev_scr[b, idx:idx + 1, :width] = x[L - 1:L, :]
            return x + (prev - x) * mu_ref[idx:idx + 1, :width]

        r = shifted(r_ref, 0, D_B)
        k = shifted(k_ref, 1, D_B)
        v = shifted(v_ref, 2, D_B)
        wa = shifted(wa_ref, 3, R_W + R_A)

        z = w0_ref[...] + _dot(jnp.tanh(wa).astype(BF16), w2_ref[...])
        log_decay = -math.exp(-0.5) * jax.nn.sigmoid(z)
        a = jax.nn.sigmoid(a0_ref[...] + _dot(wa.astype(BF16), a2_ref[...]))

        kk = k * kkw_ref[...]
        kk = kk * lax.rsqrt(jnp.maximum(head_sum(kk * kk), 1e-24))
        k_mod = k * (1.0 + (a - 1.0) * kaw_ref[...])
        bonus = head_sum(r * k_mod * rkw_ref[...]) * v

        cum = _dot(tri_incl, log_decay, lax.Precision.HIGHEST)
        cum_last = cum[L - 1:L, :]
        p_inv = jnp.exp(-cum)
        p_last = jnp.exp(cum_last)
        bet = kk * a * p_inv
        kt = k_mod * p_inv
        return dict(
            kap=cast(kk * jnp.exp(cum - log_decay)), bet=cast(bet), kt=cast(kt), rt=cast(r * jnp.exp(cum)),
            kt_tail=cast(kt * p_last), bet_tail=cast(-(bet * p_last)), vc=cast(v), p_last=p_last, bonus=bonus)

    tq = lax.broadcasted_iota(jnp.int32, (L, QUAD), 0)
    sq = lax.broadcasted_iota(jnp.int32, (L, QUAD), 1) % HEAD_B
    strict = sq < tq
    incl = sq <= tq
    eye_c = (sq == tq).astype(F32)
    blk16 = (tq // 16) == (sq // 16)
    blk32 = (tq // 32) == (sq // 32)
    strict16 = strict & blk16
    off32 = strict & blk32 & jnp.logical_not(blk16)
    off64 = strict & jnp.logical_not(blk32)

    def bd(x):
        return jnp.concatenate([x] * (QUAD // L), axis=0) * mask_bd

    def mm(x, y_bd):
        return _dot(cast(x), y_bd, prec)

    pre = [prepare(b) for b in range(nb)]

    chains = [(b, q) for b in range(nb) for q in range(N_QUAD)]
    n = range(len(chains))
    lanes = [slice(q * QUAD, (q + 1) * QUAD) for _, q in chains]
    sub = lambda name, i: pre[chains[i][0]][name][:, lanes[i]]
    st = [st_scr[b, q] for b, q in chains]
    kr = [jnp.concatenate([sub("kap", i), sub("rt", i)], axis=0) for i in n]
    g_b = [_dot_nt(kr[i], bd(sub("bet", i)), prec) for i in n]
    g_k = [_dot_nt(kr[i], bd(sub("kt", i)), prec) for i in n]
    g_s = [_dot_nt(kr[i], cast(st[i]), prec) for i in n]
    bd_v = [bd(sub("vc", i)) for i in n]

    pw = [jnp.where(strict16, -g_b[i][:L], 0.0) for i in n]
    t_inv = [eye_c + pw[i] for i in n]
    pw = [mm(pw[i], bd(cast(pw[i]))) for i in n]
    for _ in range(2):
        both = [mm(jnp.concatenate([t_inv[i], pw[i]], axis=0), bd(cast(pw[i]))) for i in n]
        t_inv = [t_inv[i] + both[i][:L] for i in n]
        pw = [both[i][L:] for i in n]
    t_inv = [t_inv[i] + mm(t_inv[i], bd(cast(pw[i]))) for i in n]
    rhs = [g_s[i][:L] + mm(jnp.where(strict, g_k[i][:L], 0.0), bd_v[i]) for i in n]
    for off_mask in (off32, off64):
        low = [mm(jnp.where(off_mask, g_b[i][:L], 0.0), bd(cast(t_inv[i]))) for i in n]
        t_inv = [t_inv[i] - mm(t_inv[i], bd(cast(low[i]))) for i in n]

    u_c = [cast(mm(t_inv[i], bd(cast(rhs[i])))) for i in n]
    ys = [g_s[i][L:] + mm(jnp.where(incl, g_k[i][L:], 0.0), bd_v[i])
          - mm(jnp.where(incl, g_b[i][L:], 0.0), bd(u_c[i])) for i in n]
    for i in n:
        b, q = chains[i]
        upd = _dot_tn(jnp.concatenate([sub("vc", i), u_c[i]], axis=0),
                      jnp.concatenate([sub("kt_tail", i), sub("bet_tail", i)], axis=0), prec)
        st_scr[b, q] = (st[i] * sub("p_last", i) + upd) * ones_head

    @pl.when(c == pl.num_programs(1) - 1)
    def _():
        gather = (lax.broadcasted_iota(jnp.int32, (QUAD, HEAD_B), 0) % HEAD_B
                  == lax.broadcasted_iota(jnp.int32, (QUAD, HEAD_B), 1)).astype(F32)
        for b in range(nb):
            for q in range(N_QUAD):
                sout_ref[b, q] = _dot(st_scr[b, q], gather, lax.Precision.HIGHEST)

    for b in range(nb):
        y = jnp.concatenate(ys[b * N_QUAD:(b + 1) * N_QUAD], axis=-1)
        mu_y = head_sum(y) * (1.0 / HEAD_B)
        yc = y - mu_y
        var_y = head_sum(yc * yc) * (1.0 / HEAD_B)
        yn = yc * lax.rsqrt(var_y + GN_EPS_B) * lng_ref[...] + lnb_ref[...]
        ob_ref[b] = (yn + pre[b]["bonus"]) * _silu(g_ref[b])


SCAN_NB = 4


def _rwkv(proj, tail, sh0, s0, prm, bsz, t, prec):
    nb = SCAN_NB
    nc = t // SCAN_L
    proj3 = proj.reshape(bsz, t, -1)
    tail3 = tail.reshape(bsz, t, -1)
    col = lambda j: pl.BlockSpec((nb, SCAN_L, D_B), lambda b, c, j=j: (b, c, j))
    full = lambda a: pl.BlockSpec(a.shape, lambda b, c: (0,) * a.ndim)
    per_b = lambda a: pl.BlockSpec((nb,) + a.shape[1:], lambda b, c: (b,) + (0,) * (a.ndim - 1))
    return pl.pallas_call(
        functools.partial(_rwkv_kernel, prec=prec),
        grid=(bsz // nb, nc),
        in_specs=[col(0), col(1), col(2), col(0),
                  pl.BlockSpec((nb, SCAN_L, R_W + R_A), lambda b, c: (b, c, D_B // (R_W + R_A))),
                  per_b(sh0), per_b(s0)] + [full(p) for p in prm],
        out_specs=[pl.BlockSpec((nb, SCAN_L, D_B), lambda b, c: (b, c, 0)), per_b(s0)],
        out_shape=[jax.ShapeDtypeStruct((bsz, t, D_B), F32), jax.ShapeDtypeStruct(s0.shape, F32)],
        scratch_shapes=[pltpu.VMEM((nb, 4, D_B), F32), pltpu.VMEM((nb, N_QUAD, QUAD, QUAD), F32)],
        name="rwkv_scan",
        compiler_params=_cparams(("parallel", "arbitrary")),
    )(proj3, proj3, proj3, tail3, tail3, sh0, s0, *prm)


def _outproj_ln_kernel(a_ref, b_ref, x_ref, w1_ref, w2_ref, g_ref, bias_ref, o_ref, *maybe_bf_ref):
    acc = _dot(a_ref[...].astype(BF16), w1_ref[...]) + _dot(b_ref[...].astype(BF16), w2_ref[...])
    y = DEEPNORM_ALPHA * x_ref[...] + acc
    mu = jnp.mean(y, -1, keepdims=True)
    yc = y - mu
    var = jnp.mean(yc * yc, -1, keepdims=True)
    out = yc * lax.rsqrt(var + LN_EPS) * g_ref[...] + bias_ref[...]
    o_ref[...] = out
    for ref in maybe_bf_ref:
        ref[...] = out.astype(BF16)


def _outproj_ln(a, a_col, b, b_col, x, w, g, bias, tm, want_bf16):
    m = x.shape[0]
    half = D_MODEL // 2
    full = lambda arr: pl.BlockSpec(arr.shape, lambda i: (0,) * arr.ndim)
    row = pl.BlockSpec((tm, D_MODEL), lambda i: (i, 0))
    n_out = 2 if want_bf16 else 1
    return pl.pallas_call(
        _outproj_ln_kernel,
        grid=(m // tm,),
        in_specs=[pl.BlockSpec((tm, half), lambda i: (i, a_col)),
                  pl.BlockSpec((tm, half), lambda i: (i, b_col)),
                  row, pl.BlockSpec((half, D_MODEL), lambda i: (0, 0)),
                  pl.BlockSpec((half, D_MODEL), lambda i: (1, 0)), full(g), full(bias)],
        out_specs=[row] * n_out,
        out_shape=[jax.ShapeDtypeStruct((m, D_MODEL), F32), jax.ShapeDtypeStruct((m, D_MODEL), BF16)][:n_out],
        name="outproj_ln",
        compiler_params=_cparams(("parallel",)),
    )(a, b, x, w, w, g, bias)


def _lambda(lq1_ref, lk1_ref, lq2_ref, lk2_ref, lam_init):
    s1 = jnp.sum(lq1_ref[...] * lk1_ref[...], -1, keepdims=True)
    s2 = jnp.sum(lq2_ref[...] * lk2_ref[...], -1, keepdims=True)
    return jnp.exp(s1) - jnp.exp(s2) + lam_init


def _attn_finish(o, g, subg_ref, lam_init):
    o = o * lax.rsqrt(jnp.mean(o * o, -1, keepdims=True) + SUBLN_EPS) * subg_ref[...]
    return o * (1.0 - lam_init) * _silu(g)


ATT_ROWS = 128
LANES = 128


def _attn_prompt_kernel(q_ref, k_ref, v_ref, g_ref, lq1_ref, lk1_ref, lq2_ref, lk2_ref, subg_ref, o_ref,
                        m0_ref, m1_ref, l0_ref, l1_ref, acc0_ref, acc1_ref, sa_ref, sb_ref, *, tq, lam_init):
    qi = pl.program_id(2)
    lam = _lambda(lq1_ref, lk1_ref, lq2_ref, lk2_ref, lam_init)
    ms = (m0_ref, m1_ref)
    ls = (l0_ref, l1_ref)
    accs = (acc0_ref, acc1_ref)
    n_sub = tq // ATT_ROWS
    n_rep = tq // LANES
    for j in range(2):
        ms[j][...] = jnp.full((tq, LANES), -jnp.inf, F32)
        ls[j][...] = jnp.zeros((tq, LANES), F32)
        accs[j][...] = jnp.zeros((tq, 2 * HEAD_C), F32)

    chains = [(r, j) for r in range(n_sub) for j in range(2)]

    def blk(i):
        return pl.multiple_of(i * tq, tq)

    def put_scores(r, j, off, s_ref):
        rows = pl.ds(r * ATT_ROWS, ATT_ROWS)
        cols = slice(j * HEAD_C, (j + 1) * HEAD_C)
        s_ref[j, rows, :] = _dot_nt(q_ref[rows, cols], k_ref[pl.ds(off, tq), cols])

    def consume(r, j, s_ref, off, masked):
        rows = pl.ds(r * ATT_ROWS, ATT_ROWS)
        nk = (r + 1) * ATT_ROWS if masked else tq
        s = s_ref[j, rows, :nk]
        if masked:
            rq = (lax.broadcasted_iota(jnp.int32, (ATT_ROWS, nk), 0) + r * ATT_ROWS) // ATT_CHUNK
            ck = lax.broadcasted_iota(jnp.int32, (ATT_ROWS, nk), 1) // ATT_CHUNK
            s = jnp.where(ck <= rq, s, -jnp.inf)
        m_old = ms[j][rows, :]
        m_new = jnp.maximum(m_old, jnp.max(s, -1, keepdims=True))
        alpha = jnp.exp2(m_old - m_new)
        p = jnp.exp2(s - jnp.concatenate([m_new] * (nk // LANES), axis=1))
        part = p[:, :LANES]
        for c in range(1, nk // LANES):
            part = part + p[:, c * LANES:(c + 1) * LANES]
        ls[j][rows, :] = alpha * ls[j][rows, :] + part
        accs[j][rows, :] = (jnp.concatenate([alpha] * 2, axis=1) * accs[j][rows, :]
                            + _dot(p.astype(BF16), v_ref[pl.ds(off, nk), :]))
        ms[j][rows, :] = m_new

    def step(s_cur, off_cur, masked, s_next=None, off_next=None):
        for r, j in chains:
            if s_next is not None:
                put_scores(r, j, off_next, s_next)
            consume(r, j, s_cur, off_cur, masked)

    for r, j in chains:
        put_scores(r, j, blk(0), sa_ref)

    def pair(p, carry):
        step(sa_ref, blk(2 * p), False, sb_ref, blk(2 * p + 1))
        step(sb_ref, blk(2 * p + 1), False, sa_ref, blk(2 * p + 2))
        return carry

    lax.fori_loop(0, qi // 2, pair, 0)

    @pl.when(qi % 2 == 1)
    def _():
        step(sa_ref, blk(qi - 1), False, sb_ref, blk(qi))
        step(sb_ref, blk(qi), True)

    @pl.when(qi % 2 == 0)
    def _():
        step(sa_ref, blk(qi), True)

    l0 = jnp.sum(l0_ref[...], -1, keepdims=True)
    l1 = jnp.sum(l1_ref[...], -1, keepdims=True)
    o = acc0_ref[...] / l0 - lam * (acc1_ref[...] / l1)
    o_ref[...] = _attn_finish(o, g_ref[...], subg_ref, lam_init)


def _attn_prompt(q, k, v, g, lam_prm, subg, bsz, t, lam_init, tq):
    nq = t // tq
    hw = 2 * HEAD_C
    full = lambda a: pl.BlockSpec(a.shape, lambda b, h, i: (0,) * a.ndim)
    tile = pl.BlockSpec((tq, hw), lambda b, h, i: (b * nq + i, h))
    whole = pl.BlockSpec((t, hw), lambda b, h, i: (b, h))
    return pl.pallas_call(
        functools.partial(_attn_prompt_kernel, tq=tq, lam_init=lam_init),
        grid=(bsz, H_C, nq),
        in_specs=[tile, whole, whole, tile] + [full(p) for p in lam_prm] + [full(subg)],
        out_specs=tile,
        out_shape=jax.ShapeDtypeStruct((bsz * t, D_C), F32),
        scratch_shapes=([pltpu.VMEM((tq, LANES), F32)] * 4 + [pltpu.VMEM((tq, hw), F32)] * 2
                        + [pltpu.VMEM((2, tq, tq), F32)] * 2),
        name="attn_prompt",
        compiler_params=_cparams(("parallel", "parallel", "arbitrary")),
    )(q, k, v, g, *lam_prm, subg)


def _attn_sample_kernel(q_ref, kn_ref, vn_ref, g_ref, kc_hbm, vc_hbm, lq1_ref, lk1_ref, lq2_ref,
                        lk2_ref, subg_ref, o_ref, kbuf, vbuf, sem, *, lam_init, past):
    step = pl.program_id(0)
    slot = step % 2

    def cache_copies(s, into):
        b = s // H_C
        h = s % H_C
        rows = pl.ds(b * past, past)
        return (pltpu.make_async_copy(kc_hbm.at[rows, 2 * h], kbuf.at[into, 0], sem.at[into, 0]),
                pltpu.make_async_copy(kc_hbm.at[rows, 2 * h + 1], kbuf.at[into, 1], sem.at[into, 1]),
                pltpu.make_async_copy(vc_hbm.at[rows, h], vbuf.at[into], sem.at[into, 2]))

    @pl.when(step == 0)
    def _():
        for cp in cache_copies(0, 0):
            cp.start()

    @pl.when(step + 1 < pl.num_programs(0))
    def _():
        for cp in cache_copies(step + 1, 1 - slot):
            cp.start()

    for cp in cache_copies(step, slot):
        cp.wait()

    lam = _lambda(lq1_ref, lk1_ref, lq2_ref, lk2_ref, lam_init)
    q = q_ref[...]
    kn = kn_ref[...]
    probs = []
    for j in range(2):
        cols = slice(j * HEAD_C, (j + 1) * HEAD_C)
        sc = _dot_nt(q[:, cols], kbuf[slot, j].astype(BF16))
        sn = _dot_nt(q[:, cols], kn[:, cols])
        m = jnp.maximum(jnp.max(sc, -1, keepdims=True), jnp.max(sn, -1, keepdims=True))
        pc = jnp.exp2(sc - m)
        pn = jnp.exp2(sn - m)
        inv = 1.0 / (jnp.sum(pc, -1, keepdims=True) + jnp.sum(pn, -1, keepdims=True))
        probs.append((pc * inv, pn * inv))
    ac = probs[0][0] - lam * probs[1][0]
    an = probs[0][1] - lam * probs[1][1]
    o = _dot(ac.astype(BF16), vbuf[slot].astype(BF16)) + _dot(an.astype(BF16), vn_ref[...])
    o_ref[...] = _attn_finish(o, g_ref[...], subg_ref, lam_init)


def _attn_sample(q, kn, vn, g, kc, vc, lam_prm, subg, bsz, t, past, lam_init):
    hw = 2 * HEAD_C
    full = lambda a: pl.BlockSpec(a.shape, lambda s: (0,) * a.ndim)
    new = pl.BlockSpec((t, hw), lambda s: (s // H_C, s % H_C))
    hbm = pl.BlockSpec(memory_space=pl.ANY)
    return pl.pallas_call(
        functools.partial(_attn_sample_kernel, lam_init=lam_init, past=past),
        grid=(bsz * H_C,),
        in_specs=[new, new, new, new, hbm, hbm] + [full(p) for p in lam_prm] + [full(subg)],
        out_specs=new,
        out_shape=jax.ShapeDtypeStruct((bsz * t, D_C), F32),
        scratch_shapes=[pltpu.VMEM((2, 2, past, HEAD_C), F32), pltpu.VMEM((2, past, hw), F32),
                        pltpu.SemaphoreType.DMA((2, 3))],
        name="attn_sample",
        compiler_params=_cparams(("arbitrary",)),
    )(q, kn, vn, g, kc, vc, *lam_prm, subg)


def _row(p):
    return p.reshape(1, -1).astype(F32)


def _pad_rows(x, width):
    lead = x.shape[:-1]
    main = x[..., :3 * width].reshape(lead + (3, width))
    rest = jnp.pad(x[..., 3 * width:], [(0, 0)] * len(lead) + [(0, 4 * width - x.shape[-1])])
    return jnp.concatenate([main, rest[..., None, :]], axis=-2)


def _ab_params(w_in, a_ln_g, a_ln_b, a_ws, a_bs, b_mu, b_w0, b_w2, b_a0, b_a2, b_kk, b_ka, b_rk, b_lnx_g,
               b_lnx_b, w_out, ln_g, ln_b):
    split = 3 * D_A + 3 * D_B
    w_main = w_in[:, :split].astype(BF16)
    w_tail = jnp.concatenate([w_in[:, split + R_W + R_A:], w_in[:, split:split + R_W + R_A]], axis=1).astype(BF16)
    zeros = jnp.zeros((R_W, D_B), F32)
    rwkv = (_pad_rows(b_mu, D_B), _row(b_w0),
            jnp.concatenate([b_w2, zeros], 0).astype(BF16), _row(b_a0),
            jnp.concatenate([zeros, b_a2], 0).astype(BF16),
            _row(b_kk), _row(b_ka), _row(b_rk), _row(b_lnx_g), _row(b_lnx_b))
    return dict(w_main=w_main, w_tail=w_tail, a_ln_g=_row(a_ln_g), a_ln_b=_row(a_ln_b), a_ws=a_ws, a_bs=a_bs,
                rwkv=rwkv, w_out=w_out.astype(BF16), ln_g=_row(ln_g), ln_b=_row(ln_b))


def _ab_layer(x, shift_prev, wkv0, prm, scan_prec, want_vn):
    bsz, t, _ = x.shape
    m = bsz * t
    x2 = x.reshape(m, D_MODEL)
    chunk = min(t, MLP_CHUNK)
    mask = jnp.tril(jnp.ones((chunk, chunk), dtype=bool))
    wm = jnp.where(mask, prm["a_ws"][:, :chunk, :chunk], 0).astype(BF16)
    bias = jnp.repeat(jnp.swapaxes(prm["a_bs"][:, :chunk], 0, 1), C_A, axis=1).astype(F32)
    proj, tail, out_a, *maybe_vn = _ab_proj(x2, prm["w_main"], prm["w_tail"], prm["a_ln_g"], prm["a_ln_b"], wm, bias,
                                            min(m, 512), chunk, want_vn)

    sh0 = _pad_rows(shift_prev.astype(F32), D_B)
    s0 = wkv0.astype(F32).reshape(bsz, N_QUAD, QUAD, HEAD_B)
    out_b, st_new = _rwkv(proj, tail, sh0, s0, prm["rwkv"], bsz, t, scan_prec)

    x_new, x_new_bf = _outproj_ln(out_a, 0, out_b.reshape(m, D_B), 0, x2, prm["w_out"], prm["ln_g"], prm["ln_b"],
                                  min(m, 512), True)
    last = proj.reshape(bsz, t, -1)[:, -1]
    shift_new = jnp.concatenate([last, tail.reshape(bsz, t, -1)[:, -1, D_B:]], axis=-1)
    v_n = maybe_vn[0].reshape(bsz, t, D_A) if want_vn else None
    return (x_new, x_new_bf), shift_new, st_new.reshape(bsz, H_B, HEAD_B, HEAD_B), v_n


def _c_layer(x2, xb, bsz, t, k_cache, v_cache, lam_init, w_in, lam_prm, subg, w_out, ln_g, ln_b):
    m = bsz * t
    tm = min(m, 512)
    q, = _cproj(xb, w_in, 0, "q", tm)
    k_leaf, kb = _cproj(xb, w_in, 1, "k", tm)
    v_leaf, vb = _cproj(xb, w_in, 2, "v", tm)
    g, = _cproj(xb, w_in, 3, "g", tm)
    if k_cache is None:
        o = _attn_prompt(q, kb, vb, g, lam_prm, subg, bsz, t, lam_init, 512)
    else:
        past = k_cache.shape[1]
        o = _attn_sample(q, kb, vb, g, k_cache.reshape(bsz * past, K_ROWS, HEAD_C),
                         v_cache.reshape(bsz * past, H_C, 2 * HEAD_C), lam_prm, subg, bsz, t, past, lam_init)
    x_new, = _outproj_ln(o, 0, o, 1, x2, w_out, ln_g, ln_b, min(m, 512), False)
    k_new = k_leaf.reshape(bsz, t, H_C, 2, HEAD_C)
    v_new = v_leaf.reshape(bsz, t, H_C, 2 * HEAD_C)
    return x_new.reshape(bsz, t, D_MODEL), k_new, v_new


def kernel(x_prompt, x_sample, state_b_shift, state_b_wkv, cache_c_k, cache_c_v, ab_w_in, ab_a_ln_g, ab_a_ln_b, ab_a_ws, ab_a_bs, ab_b_mu, ab_b_w0, ab_b_w2, ab_b_a0, ab_b_a2, ab_b_kk, ab_b_ka, ab_b_rk, ab_b_lnx_g, ab_b_lnx_b, ab_w_out, ab_ln_g, ab_ln_b, c_w_in, c_lam_q1, c_lam_k1, c_lam_q2, c_lam_k2, c_subln_g, c_w_out, c_ln_g, c_ln_b):
    scan_prec = None
    bp, tp = x_prompt.shape[:2]
    bs, ts = x_sample.shape[:2]
    x_p, x_s = x_prompt, x_sample
    xp_pair = xs_pair = None
    sh_p_l, wkv_p_l, sh_s_l, wkv_s_l, va_s_l = [], [], [], [], []
    kp_l, vp_l, ks_l, vs_l = [], [], [], []
    for li in range(DEPTH):
        j = li // 2
        if li % 2 == 0:
            prm = _ab_params(ab_w_in[j], ab_a_ln_g[j], ab_a_ln_b[j], ab_a_ws[j], ab_a_bs[j], ab_b_mu[j],
                             ab_b_w0[j], ab_b_w2[j], ab_b_a0[j], ab_b_a2[j], ab_b_kk[j], ab_b_ka[j],
                             ab_b_rk[j], ab_b_lnx_g[j], ab_b_lnx_b[j], ab_w_out[j], ab_ln_g[j], ab_ln_b[j])
            shift0 = jnp.zeros((bp, D_B_SHIFT), F32)
            wkv0 = jnp.zeros((bp, H_B, HEAD_B, HEAD_B), F32)
            xp_pair, sh_p, wkv_p, _ = _ab_layer(x_p, shift0, wkv0, prm, scan_prec, False)
            xs_pair, sh_s, wkv_s, va_s = _ab_layer(x_s, state_b_shift[j], state_b_wkv[j], prm, scan_prec, True)
            sh_p_l.append(sh_p)
            wkv_p_l.append(wkv_p)
            sh_s_l.append(sh_s)
            wkv_s_l.append(wkv_s)
            va_s_l.append(va_s)
        else:
            lam_init = 0.8 - 0.6 * math.exp(-0.3 * li)
            lam_prm = (_row(c_lam_q1[j]), _row(c_lam_k1[j]), _row(c_lam_q2[j]), _row(c_lam_k2[j]))
            args = (lam_init, c_w_in[j].astype(BF16), lam_prm, _row(c_subln_g[j]), c_w_out[j].astype(BF16), _row(c_ln_g[j]),
                    _row(c_ln_b[j]))
            x_p, k_p, v_p = _c_layer(*xp_pair, bp, tp, None, None, *args)
            x_s, k_s, v_s = _c_layer(*xs_pair, bs, ts, cache_c_k[j], cache_c_v[j], *args)
            kp_l.append(k_p)
            vp_l.append(v_p)
            ks_l.append(k_s)
            vs_l.append(v_s)
    return (x_p, x_s, jnp.stack(sh_p_l), jnp.stack(wkv_p_l), jnp.stack(sh_s_l), jnp.stack(wkv_s_l),
            jnp.stack(va_s_l), jnp.stack(kp_l), jnp.stack(vp_l), jnp.stack(ks_l), jnp.stack(vs_l))
```

```python
import functools
import math

import jax
import jax.numpy as jnp
from jax import lax
from jax.experimental import pallas as pl
from jax.experimental.pallas import tpu as pltpu

F32 = jnp.float32
BF16 = jnp.bfloat16

D_MODEL = 2048
DEPTH = 2
D_A = 1024
G_A = 4
C_A = D_A // G_A
MLP_CHUNK = 128
D_B = 1024
HEAD_B = 64
H_B = D_B // HEAD_B
R_W = 64
R_A = 64
D_B_SHIFT = 3 * D_B + R_W + R_A
HEAD_C = 128
H_C = 8
D_C = 2048
ATT_CHUNK = 64
DEEPNORM_ALPHA = (2 * DEPTH) ** 0.25
LN_EPS = 1e-5
GN_EPS_B = 64e-5
SUBLN_EPS = 1e-5

QUAD = 4 * HEAD_B
N_QUAD = D_B // QUAD
SCAN_L = 64
VMEM_LIMIT = 56 * 1024 * 1024


def _cparams(sem):
    return pltpu.CompilerParams(dimension_semantics=sem, vmem_limit_bytes=VMEM_LIMIT)


def _dot(a, b, prec=None):
    return jnp.dot(a, b, precision=prec, preferred_element_type=F32)


def _dot_nt(a, b, prec=None):
    return lax.dot_general(a, b, (((1,), (1,)), ((), ())), precision=prec, preferred_element_type=F32)


def _dot_tn(a, b, prec=None):
    return lax.dot_general(a, b, (((0,), (0,)), ((), ())), precision=prec, preferred_element_type=F32)


def _silu(x):
    return x * jax.nn.sigmoid(x)


N_A_COLS = 3


def _abproj_kernel(x_ref, w_ref, wt_ref, lng_ref, lnb_ref, wm_ref, bias_ref, o_ref, ot_ref, oa_ref, *rest, chunk):
    *maybe_vn_ref, xb_ref, u_scr, v_scr, ga_scr, vn_scr = rest
    j = pl.program_id(1)

    def block():
        return _dot(xb_ref[...], w_ref[...])

    @pl.when(j == 0)
    def _():
        xb = x_ref[...].astype(BF16)
        xb_ref[...] = xb
        ot_ref[...] = _dot(xb, wt_ref[...])
        u_scr[...] = block()

    @pl.when(j == 1)
    def _():
        v_scr[...] = block()

    @pl.when(j == 2)
    def _():
        ga_scr[...] = block()
        v = v_scr[...]
        mu = jnp.mean(v, -1, keepdims=True)
        vc = v - mu
        var = jnp.mean(vc * vc, -1, keepdims=True)
        vn = vc * lax.rsqrt(var + LN_EPS) * lng_ref[...] + lnb_ref[...]
        for ref in maybe_vn_ref:
            ref[...] = vn
        vn_scr[...] = vn.astype(BF16)

    @pl.when(j == N_A_COLS)
    def _():
        o_ref[...] = block()
        for c in range(u_scr.shape[0] // chunk):
            rows = slice(c * chunk, (c + 1) * chunk)
            parts = [_dot(wm_ref[g], vn_scr[rows, g * C_A:(g + 1) * C_A]) for g in range(G_A)]
            sg = jnp.concatenate(parts, axis=-1) + bias_ref[...]
            oa_ref[rows, :] = (u_scr[rows, :] * sg * _silu(ga_scr[rows, :])).astype(oa_ref.dtype)

    @pl.when(j > N_A_COLS)
    def _():
        o_ref[...] = block()


def _ab_proj(x, w, w_tail, lng, lnb, wm, bias, tm, chunk, want_vn):
    m, k = x.shape
    n_col = w.shape[1] // D_A
    nt = w_tail.shape[1]
    full = lambda a: pl.BlockSpec(a.shape, lambda i, j: (0,) * a.ndim)
    row = lambda width: pl.BlockSpec((tm, width), lambda i, j: (i, 0))
    n_vn = 1 if want_vn else 0
    return pl.pallas_call(
        functools.partial(_abproj_kernel, chunk=chunk),
        grid=(m // tm, n_col),
        in_specs=[pl.BlockSpec((tm, k), lambda i, j: (i, 0)),
                  pl.BlockSpec((k, D_A), lambda i, j: (0, j)),
                  full(w_tail), full(lng), full(lnb), full(wm), full(bias)],
        out_specs=[pl.BlockSpec((tm, D_A), lambda i, j: (i, jnp.maximum(j - N_A_COLS, 0))),
                   row(nt), row(D_A)] + [row(D_A)] * n_vn,
        out_shape=[jax.ShapeDtypeStruct((m, (n_col - N_A_COLS) * D_A), F32), jax.ShapeDtypeStruct((m, nt), F32),
                   jax.ShapeDtypeStruct((m, D_A), BF16)] + [jax.ShapeDtypeStruct((m, D_A), F32)] * n_vn,
        scratch_shapes=[pltpu.VMEM((tm, k), BF16)] + [pltpu.VMEM((tm, D_A), F32)] * 3 + [pltpu.VMEM((tm, D_A), BF16)],
        name="ab_proj",
        compiler_params=_cparams(("parallel", "arbitrary")),
    )(x, w, w_tail, lng, lnb, wm, bias)


QK_SCALE = (HEAD_C ** -0.5) * math.log2(math.e)
K_ROWS = 2 * H_C


def _cproj_kernel(x_ref, w_ref, *o_refs, mode):
    acc = _dot(x_ref[...], w_ref[...])
    if mode == "q":
        o_refs[0][...] = (acc * QK_SCALE).astype(BF16)
    elif mode == "k":
        leaf_hbm, bf_ref, stage, sem = o_refs
        i = pl.program_id(0)
        n_steps = pl.num_programs(0)
        tm = acc.shape[0]
        slot = i % 2

        def leaf_copies(s, src):
            rows = pl.ds(s * tm, tm)
            return [pltpu.make_async_copy(stage.at[src, :, pl.ds(c * HEAD_C, HEAD_C)], leaf_hbm.at[rows, c],
                                          sem.at[src]) for c in range(K_ROWS)]

        bf_ref[...] = acc.astype(BF16)

        @pl.when(i >= 2)
        def _():
            for cp in leaf_copies(i - 2, slot):
                cp.wait()

        stage[slot] = acc
        for cp in leaf_copies(i, slot):
            cp.start()

        @pl.when(i == n_steps - 1)
        def _():
            @pl.when(i >= 1)
            def _():
                for cp in leaf_copies(i - 1, 1 - slot):
                    cp.wait()

            for cp in leaf_copies(i, slot):
                cp.wait()
    elif mode == "v":
        leaf_ref, bf_ref = o_refs
        leaf_ref[...] = acc
        bf_ref[...] = acc.astype(BF16)
    else:
        o_refs[0][...] = acc


def _cproj(xb, w, col, mode, tm):
    m, k = xb.shape
    n = D_C
    flat = lambda dt: (pl.BlockSpec((tm, n), lambda i: (i, 0)), jax.ShapeDtypeStruct((m, n), dt))
    if mode == "q":
        outs = [flat(BF16)]
    elif mode == "k":
        outs = [(pl.BlockSpec(memory_space=pl.ANY), jax.ShapeDtypeStruct((m, K_ROWS, HEAD_C), F32)), flat(BF16)]
    elif mode == "v":
        outs = [flat(F32), flat(BF16)]
    else:
        outs = [flat(F32)]
    scratch = [pltpu.VMEM((2, tm, n), F32), pltpu.SemaphoreType.DMA((2,))] if mode == "k" else []
    return pl.pallas_call(
        functools.partial(_cproj_kernel, mode=mode),
        grid=(m // tm,),
        in_specs=[pl.BlockSpec((tm, k), lambda i: (i, 0)), pl.BlockSpec((k, n), lambda i: (0, col))],
        out_specs=[o[0] for o in outs],
        out_shape=[o[1] for o in outs],
        scratch_shapes=scratch,
        name="cproj_" + mode,
        compiler_params=_cparams(("arbitrary" if mode == "k" else "parallel",)),
    )(xb, w)


def _rwkv_kernel(r_ref, k_ref, v_ref, g_ref, wa_ref, sh0_ref, s0_ref, mu_ref, w0_ref, w2_ref, a0_ref,
                 a2_ref, kkw_ref, kaw_ref, rkw_ref, lng_ref, lnb_ref, ob_ref, sout_ref, prev_scr, st_scr,
                 *, prec):
    L = SCAN_L
    nb = r_ref.shape[0]
    c = pl.program_id(1)

    row4 = lax.broadcasted_iota(jnp.int32, (QUAD, QUAD), 0)
    col4 = lax.broadcasted_iota(jnp.int32, (QUAD, QUAD), 1)
    same_head = (row4 // HEAD_B) == (col4 // HEAD_B)
    ones_head = same_head.astype(F32)

    @pl.when(c == 0)
    def _():
        prev_scr[...] = sh0_ref[...]
        spread = (lax.broadcasted_iota(jnp.int32, (HEAD_B, QUAD), 1) % HEAD_B
                  == lax.broadcasted_iota(jnp.int32, (HEAD_B, QUAD), 0)).astype(F32)
        for b in range(nb):
            for q in range(N_QUAD):
                st_scr[b, q] = _dot(s0_ref[b, q], spread, lax.Precision.HIGHEST) * ones_head
    ones_bf = same_head.astype(BF16)
    fast = prec is None
    mask_bd = ones_bf if fast else ones_head

    def cast(x):
        return x.astype(BF16) if fast else x

    def split2(x):
        hi = x.astype(BF16)
        return jnp.concatenate([hi, (x - hi.astype(F32)).astype(BF16)], axis=0)

    def head_sum(x):
        parts = []
        for q in range(N_QUAD):
            s = _dot(split2(x[:, q * QUAD:(q + 1) * QUAD]), ones_bf)
            parts.append(s[:L] + s[L:])
        return jnp.concatenate(parts, axis=-1)

    tl = lax.broadcasted_iota(jnp.int32, (L, L), 0)
    sl = lax.broadcasted_iota(jnp.int32, (L, L), 1)
    tri_incl = (sl <= tl).astype(F32)

    def prepare(b):
        def shifted(ref, idx, width):
            x = ref[b]
            first = lax.broadcasted_iota(jnp.int32, (L, width), 0) == 0
            prev = jnp.where(first, prev_scr[b, idx:idx + 1, :width], pltpu.roll(x, 1, 0))
            prev_scr[b, idx:idx + 1, :width] = x[L - 1:L, :]
            return x + (prev - x) * mu_ref[idx:idx + 1, :width]

        r = shifted(r_ref, 0, D_B)
        k = shifted(k_ref, 1, D_B)
        v = shifted(v_ref, 2, D_B)
        wa = shifted(wa_ref, 3, R_W + R_A)

        z = w0_ref[...] + _dot(jnp.tanh(wa).astype(BF16), w2_ref[...])
        log_decay = -math.exp(-0.5) * jax.nn.sigmoid(z)
        a = jax.nn.sigmoid(a0_ref[...] + _dot(wa.astype(BF16), a2_ref[...]))

        kk = k * kkw_ref[...]
        kk = kk * lax.rsqrt(jnp.maximum(head_sum(kk * kk), 1e-24))
        k_mod = k * (1.0 + (a - 1.0) * kaw_ref[...])
        bonus = head_sum(r * k_mod * rkw_ref[...]) * v

        cum = _dot(tri_incl, log_decay, lax.Precision.HIGHEST)
        cum_last = cum[L - 1:L, :]
        p_inv = jnp.exp(-cum)
        p_last = jnp.exp(cum_last)
        bet = kk * a * p_inv
        kt = k_mod * p_inv
        return dict(
            kap=cast(kk * jnp.exp(cum - log_decay)), bet=cast(bet), kt=cast(kt), rt=cast(r * jnp.exp(cum)),
            kt_tail=cast(kt * p_last), bet_tail=cast(-(bet * p_last)), vc=cast(v), p_last=p_last, bonus=bonus)

    tq = lax.broadcasted_iota(jnp.int32, (L, QUAD), 0)
    sq = lax.broadcasted_iota(jnp.int32, (L, QUAD), 1) % HEAD_B
    strict = sq < tq
    incl = sq <= tq
    eye_c = (sq == tq).astype(F32)
    blk16 = (tq // 16) == (sq // 16)
    blk32 = (tq // 32) == (sq // 32)
    strict16 = strict & blk16
    off32 = strict & blk32 & jnp.logical_not(blk16)
    off64 = strict & jnp.logical_not(blk32)

    def bd(x):
        return jnp.concatenate([x] * (QUAD // L), axis=0) * mask_bd

    def mm(x, y_bd):
        return _dot(cast(x), y_bd, prec)

    pre = [prepare(b) for b in range(nb)]

    chains = [(b, q) for b in range(nb) for q in range(N_QUAD)]
    n = range(len(chains))
    lanes = [slice(q * QUAD, (q + 1) * QUAD) for _, q in chains]
    sub = lambda name, i: pre[chains[i][0]][name][:, lanes[i]]
    st = [st_scr[b, q] for b, q in chains]
    kr = [jnp.concatenate([sub("kap", i), sub("rt", i)], axis=0) for i in n]
    g_b = [_dot_nt(kr[i], bd(sub("bet", i)), prec) for i in n]
    g_k = [_dot_nt(kr[i], bd(sub("kt", i)), prec) for i in n]
    g_s = [_dot_nt(kr[i], cast(st[i]), prec) for i in n]
    bd_v = [bd(sub("vc", i)) for i in n]

    pw = [jnp.where(strict16, -g_b[i][:L], 0.0) for i in n]
    t_inv = [eye_c + pw[i] for i in n]
    pw = [mm(pw[i], bd(cast(pw[i]))) for i in n]
    for _ in range(2):
        both = [mm(jnp.concatenate([t_inv[i], pw[i]], axis=0), bd(cast(pw[i]))) for i in n]
        t_inv = [t_inv[i] + both[i][:L] for i in n]
        pw = [both[i][L:] for i in n]
    t_inv = [t_inv[i] + mm(t_inv[i], bd(cast(pw[i]))) for i in n]
    rhs = [g_s[i][:L] + mm(jnp.where(strict, g_k[i][:L], 0.0), bd_v[i]) for i in n]
    for off_mask in (off32, off64):
        low = [mm(jnp.where(off_mask, g_b[i][:L], 0.0), bd(cast(t_inv[i]))) for i in n]
        t_inv = [t_inv[i] - mm(t_inv[i], bd(cast(low[i]))) for i in n]

    u_c = [cast(mm(t_inv[i], bd(cast(rhs[i])))) for i in n]
    ys = [g_s[i][L:] + mm(jnp.where(incl, g_k[i][L:], 0.0), bd_v[i])
          - mm(jnp.where(incl, g_b[i][L:], 0.0), bd(u_c[i])) for i in n]
    for i in n:
        b, q = chains[i]
        upd = _dot_tn(jnp.concatenate([sub("vc", i), u_c[i]], axis=0),
                      jnp.concatenate([sub("kt_tail", i), sub("bet_tail", i)], axis=0), prec)
        st_scr[b, q] = (st[i] * sub("p_last", i) + upd) * ones_head

    @pl.when(c == pl.num_programs(1) - 1)
    def _():
        gather = (lax.broadcasted_iota(jnp.int32, (QUAD, HEAD_B), 0) % HEAD_B
                  == lax.broadcasted_iota(jnp.int32, (QUAD, HEAD_B), 1)).astype(F32)
        for b in range(nb):
            for q in range(N_QUAD):
                sout_ref[b, q] = _dot(st_scr[b, q], gather, lax.Precision.HIGHEST)

    for b in range(nb):
        y = jnp.concatenate(ys[b * N_QUAD:(b + 1) * N_QUAD], axis=-1)
        mu_y = head_sum(y) * (1.0 / HEAD_B)
        yc = y - mu_y
        var_y = head_sum(yc * yc) * (1.0 / HEAD_B)
        yn = yc * lax.rsqrt(var_y + GN_EPS_B) * lng_ref[...] + lnb_ref[...]
        ob_ref[b] = ((yn + pre[b]["bonus"]) * _silu(g_ref[b])).astype(ob_ref.dtype)


SCAN_NB = 4


def _rwkv(proj, tail, sh0, s0, prm, bsz, t, prec):
    nb = SCAN_NB
    nc = t // SCAN_L
    proj3 = proj.reshape(bsz, t, -1)
    tail3 = tail.reshape(bsz, t, -1)
    col = lambda j: pl.BlockSpec((nb, SCAN_L, D_B), lambda b, c, j=j: (b, c, j))
    full = lambda a: pl.BlockSpec(a.shape, lambda b, c: (0,) * a.ndim)
    per_b = lambda a: pl.BlockSpec((nb,) + a.shape[1:], lambda b, c: (b,) + (0,) * (a.ndim - 1))
    return pl.pallas_call(
        functools.partial(_rwkv_kernel, prec=prec),
        grid=(bsz // nb, nc),
        in_specs=[col(0), col(1), col(2), col(0),
                  pl.BlockSpec((nb, SCAN_L, R_W + R_A), lambda b, c: (b, c, D_B // (R_W + R_A))),
                  per_b(sh0), per_b(s0)] + [full(p) for p in prm],
        out_specs=[pl.BlockSpec((nb, SCAN_L, D_B), lambda b, c: (b, c, 0)), per_b(s0)],
        out_shape=[jax.ShapeDtypeStruct((bsz, t, D_B), BF16), jax.ShapeDtypeStruct(s0.shape, F32)],
        scratch_shapes=[pltpu.VMEM((nb, 4, D_B), F32), pltpu.VMEM((nb, N_QUAD, QUAD, QUAD), F32)],
        name="rwkv_scan",
        compiler_params=_cparams(("parallel", "arbitrary")),
    )(proj3, proj3, proj3, tail3, tail3, sh0, s0, *prm)


def _outproj_ln_kernel(a_ref, b_ref, x_ref, w1_ref, w2_ref, g_ref, bias_ref, o_ref, *maybe_bf_ref):
    acc = _dot(a_ref[...], w1_ref[...]) + _dot(b_ref[...], w2_ref[...])
    y = DEEPNORM_ALPHA * x_ref[...] + acc
    mu = jnp.mean(y, -1, keepdims=True)
    yc = y - mu
    var = jnp.mean(yc * yc, -1, keepdims=True)
    out = yc * lax.rsqrt(var + LN_EPS) * g_ref[...] + bias_ref[...]
    o_ref[...] = out
    for ref in maybe_bf_ref:
        ref[...] = out.astype(BF16)


def _outproj_ln(a, a_col, b, b_col, x, w, g, bias, tm, want_bf16):
    m = x.shape[0]
    half = D_MODEL // 2
    full = lambda arr: pl.BlockSpec(arr.shape, lambda i: (0,) * arr.ndim)
    row = pl.BlockSpec((tm, D_MODEL), lambda i: (i, 0))
    n_out = 2 if want_bf16 else 1
    return pl.pallas_call(
        _outproj_ln_kernel,
        grid=(m // tm,),
        in_specs=[pl.BlockSpec((tm, half), lambda i: (i, a_col)),
                  pl.BlockSpec((tm, half), lambda i: (i, b_col)),
                  row, pl.BlockSpec((half, D_MODEL), lambda i: (0, 0)),
                  pl.BlockSpec((half, D_MODEL), lambda i: (1, 0)), full(g), full(bias)],
        out_specs=[row] * n_out,
        out_shape=[jax.ShapeDtypeStruct((m, D_MODEL), F32), jax.ShapeDtypeStruct((m, D_MODEL), BF16)][:n_out],
        name="outproj_ln",
        compiler_params=_cparams(("parallel",)),
    )(a, b, x, w, w, g, bias)


def _lambda(lq1_ref, lk1_ref, lq2_ref, lk2_ref, lam_init):
    s1 = jnp.sum(lq1_ref[...] * lk1_ref[...], -1, keepdims=True)
    s2 = jnp.sum(lq2_ref[...] * lk2_ref[...], -1, keepdims=True)
    return jnp.exp(s1) - jnp.exp(s2) + lam_init


def _attn_finish(o, g, subg_ref, lam_init):
    o = o * lax.rsqrt(jnp.mean(o * o, -1, keepdims=True) + SUBLN_EPS) * subg_ref[...]
    return o * (1.0 - lam_init) * _silu(g)


ATT_ROWS = 128
LANES = 128


def _attn_prompt_kernel(q_ref, k_ref, v_ref, g_ref, lq1_ref, lk1_ref, lq2_ref, lk2_ref, subg_ref, o_ref,
                        m0_ref, m1_ref, l0_ref, l1_ref, acc0_ref, acc1_ref, sa_ref, sb_ref, *, tq, lam_init):
    qi = pl.program_id(2)
    lam = _lambda(lq1_ref, lk1_ref, lq2_ref, lk2_ref, lam_init)
    ms = (m0_ref, m1_ref)
    ls = (l0_ref, l1_ref)
    accs = (acc0_ref, acc1_ref)
    n_sub = tq // ATT_ROWS
    n_rep = tq // LANES
    for j in range(2):
        ms[j][...] = jnp.full((tq, LANES), -jnp.inf, F32)
        ls[j][...] = jnp.zeros((tq, LANES), F32)
        accs[j][...] = jnp.zeros((tq, 2 * HEAD_C), F32)

    chains = [(r, j) for r in range(n_sub) for j in range(2)]

    def blk(i):
        return pl.multiple_of(i * tq, tq)

    def put_scores(r, j, off, s_ref):
        rows = pl.ds(r * ATT_ROWS, ATT_ROWS)
        cols = slice(j * HEAD_C, (j + 1) * HEAD_C)
        s_ref[j, rows, :] = _dot_nt(q_ref[rows, cols], k_ref[pl.ds(off, tq), cols])

    def consume(r, j, s_ref, off, masked):
        rows = pl.ds(r * ATT_ROWS, ATT_ROWS)
        nk = (r + 1) * ATT_ROWS if masked else tq
        s = s_ref[j, rows, :nk]
        if masked:
            rq = (lax.broadcasted_iota(jnp.int32, (ATT_ROWS, nk), 0) + r * ATT_ROWS) // ATT_CHUNK
            ck = lax.broadcasted_iota(jnp.int32, (ATT_ROWS, nk), 1) // ATT_CHUNK
            s = jnp.where(ck <= rq, s, -jnp.inf)
        m_old = ms[j][rows, :]
        m_new = jnp.maximum(m_old, jnp.max(s, -1, keepdims=True))
        alpha = jnp.exp2(m_old - m_new)
        p = jnp.exp2(s - jnp.concatenate([m_new] * (nk // LANES), axis=1))
        part = p[:, :LANES]
        for c in range(1, nk // LANES):
            part = part + p[:, c * LANES:(c + 1) * LANES]
        ls[j][rows, :] = alpha * ls[j][rows, :] + part
        accs[j][rows, :] = (jnp.concatenate([alpha] * 2, axis=1) * accs[j][rows, :]
                            + _dot(p.astype(BF16), v_ref[pl.ds(off, nk), :]))
        ms[j][rows, :] = m_new

    def step(s_cur, off_cur, masked, s_next=None, off_next=None):
        for r, j in chains:
            if s_next is not None:
                put_scores(r, j, off_next, s_next)
            consume(r, j, s_cur, off_cur, masked)

    for r, j in chains:
        put_scores(r, j, blk(0), sa_ref)

    def pair(p, carry):
        step(sa_ref, blk(2 * p), False, sb_ref, blk(2 * p + 1))
        step(sb_ref, blk(2 * p + 1), False, sa_ref, blk(2 * p + 2))
        return carry

    lax.fori_loop(0, qi // 2, pair, 0)

    @pl.when(qi % 2 == 1)
    def _():
        step(sa_ref, blk(qi - 1), False, sb_ref, blk(qi))
        step(sb_ref, blk(qi), True)

    @pl.when(qi % 2 == 0)
    def _():
        step(sa_ref, blk(qi), True)

    l0 = jnp.sum(l0_ref[...], -1, keepdims=True)
    l1 = jnp.sum(l1_ref[...], -1, keepdims=True)
    o = acc0_ref[...] / l0 - lam * (acc1_ref[...] / l1)
    o_ref[...] = _attn_finish(o, g_ref[...], subg_ref, lam_init).astype(o_ref.dtype)


def _attn_prompt(q, k, v, g, lam_prm, subg, bsz, t, lam_init, tq):
    nq = t // tq
    hw = 2 * HEAD_C
    full = lambda a: pl.BlockSpec(a.shape, lambda b, h, i: (0,) * a.ndim)
    tile = pl.BlockSpec((tq, hw), lambda b, h, i: (b * nq + i, h))
    whole = pl.BlockSpec((t, hw), lambda b, h, i: (b, h))
    return pl.pallas_call(
        functools.partial(_attn_prompt_kernel, tq=tq, lam_init=lam_init),
        grid=(bsz, H_C, nq),
        in_specs=[tile, whole, whole, tile] + [full(p) for p in lam_prm] + [full(subg)],
        out_specs=tile,
        out_shape=jax.ShapeDtypeStruct((bsz * t, D_C), BF16),
        scratch_shapes=([pltpu.VMEM((tq, LANES), F32)] * 4 + [pltpu.VMEM((tq, hw), F32)] * 2
                        + [pltpu.VMEM((2, tq, tq), F32)] * 2),
        name="attn_prompt",
        compiler_params=_cparams(("parallel", "parallel", "arbitrary")),
    )(q, k, v, g, *lam_prm, subg)


def _attn_sample_kernel(q_ref, kn_ref, vn_ref, g_ref, kc_hbm, vc_hbm, lq1_ref, lk1_ref, lq2_ref,
                        lk2_ref, subg_ref, o_ref, kbuf, vbuf, sem, *, lam_init, past):
    step = pl.program_id(0)
    slot = step % 2

    def cache_copies(s, into):
        b = s // H_C
        h = s % H_C
        rows = pl.ds(b * past, past)
        return (pltpu.make_async_copy(kc_hbm.at[rows, 2 * h], kbuf.at[into, 0], sem.at[into, 0]),
                pltpu.make_async_copy(kc_hbm.at[rows, 2 * h + 1], kbuf.at[into, 1], sem.at[into, 1]),
                pltpu.make_async_copy(vc_hbm.at[rows, h], vbuf.at[into], sem.at[into, 2]))

    @pl.when(step == 0)
    def _():
        for cp in cache_copies(0, 0):
            cp.start()

    @pl.when(step + 1 < pl.num_programs(0))
    def _():
        for cp in cache_copies(step + 1, 1 - slot):
            cp.start()

    for cp in cache_copies(step, slot):
        cp.wait()

    lam = _lambda(lq1_ref, lk1_ref, lq2_ref, lk2_ref, lam_init)
    q = q_ref[...]
    kn = kn_ref[...]
    probs = []
    for j in range(2):
        cols = slice(j * HEAD_C, (j + 1) * HEAD_C)
        sc = _dot_nt(q[:, cols], kbuf[slot, j].astype(BF16))
        sn = _dot_nt(q[:, cols], kn[:, cols])
        m = jnp.maximum(jnp.max(sc, -1, keepdims=True), jnp.max(sn, -1, keepdims=True))
        pc = jnp.exp2(sc - m)
        pn = jnp.exp2(sn - m)
        inv = 1.0 / (jnp.sum(pc, -1, keepdims=True) + jnp.sum(pn, -1, keepdims=True))
        probs.append((pc * inv, pn * inv))
    ac = probs[0][0] - lam * probs[1][0]
    an = probs[0][1] - lam * probs[1][1]
    o = _dot(ac.astype(BF16), vbuf[slot].astype(BF16)) + _dot(an.astype(BF16), vn_ref[...])
    o_ref[...] = _attn_finish(o, g_ref[...], subg_ref, lam_init).astype(o_ref.dtype)


def _attn_sample(q, kn, vn, g, kc, vc, lam_prm, subg, bsz, t, past, lam_init):
    hw = 2 * HEAD_C
    full = lambda a: pl.BlockSpec(a.shape, lambda s: (0,) * a.ndim)
    new = pl.BlockSpec((t, hw), lambda s: (s // H_C, s % H_C))
    hbm = pl.BlockSpec(memory_space=pl.ANY)
    return pl.pallas_call(
        functools.partial(_attn_sample_kernel, lam_init=lam_init, past=past),
        grid=(bsz * H_C,),
        in_specs=[new, new, new, new, hbm, hbm] + [full(p) for p in lam_prm] + [full(subg)],
        out_specs=new,
        out_shape=jax.ShapeDtypeStruct((bsz * t, D_C), BF16),
        scratch_shapes=[pltpu.VMEM((2, 2, past, HEAD_C), F32), pltpu.VMEM((2, past, hw), F32),
                        pltpu.SemaphoreType.DMA((2, 3))],
        name="attn_sample",
        compiler_params=_cparams(("arbitrary",)),
    )(q, kn, vn, g, kc, vc, *lam_prm, subg)


def _row(p):
    return p.reshape(1, -1).astype(F32)


def _pad_rows(x, width):
    lead = x.shape[:-1]
    main = x[..., :3 * width].reshape(lead + (3, width))
    rest = jnp.pad(x[..., 3 * width:], [(0, 0)] * len(lead) + [(0, 4 * width - x.shape[-1])])
    return jnp.concatenate([main, rest[..., None, :]], axis=-2)


def _ab_params(w_in, a_ln_g, a_ln_b, a_ws, a_bs, b_mu, b_w0, b_w2, b_a0, b_a2, b_kk, b_ka, b_rk, b_lnx_g,
               b_lnx_b, w_out, ln_g, ln_b):
    split = 3 * D_A + 3 * D_B
    w_main = w_in[:, :split].astype(BF16)
    w_tail = jnp.concatenate([w_in[:, split + R_W + R_A:], w_in[:, split:split + R_W + R_A]], axis=1).astype(BF16)
    zeros = jnp.zeros((R_W, D_B), F32)
    rwkv = (_pad_rows(b_mu, D_B), _row(b_w0),
            jnp.concatenate([b_w2, zeros], 0).astype(BF16), _row(b_a0),
            jnp.concatenate([zeros, b_a2], 0).astype(BF16),
            _row(b_kk), _row(b_ka), _row(b_rk), _row(b_lnx_g), _row(b_lnx_b))
    return dict(w_main=w_main, w_tail=w_tail, a_ln_g=_row(a_ln_g), a_ln_b=_row(a_ln_b), a_ws=a_ws, a_bs=a_bs,
                rwkv=rwkv, w_out=w_out.astype(BF16), ln_g=_row(ln_g), ln_b=_row(ln_b))


def _ab_layer(x, shift_prev, wkv0, prm, scan_prec, want_vn):
    bsz, t, _ = x.shape
    m = bsz * t
    x2 = x.reshape(m, D_MODEL)
    chunk = min(t, MLP_CHUNK)
    mask = jnp.tril(jnp.ones((chunk, chunk), dtype=bool))
    wm = jnp.where(mask, prm["a_ws"][:, :chunk, :chunk], 0).astype(BF16)
    bias = jnp.repeat(jnp.swapaxes(prm["a_bs"][:, :chunk], 0, 1), C_A, axis=1).astype(F32)
    proj, tail, out_a, *maybe_vn = _ab_proj(x2, prm["w_main"], prm["w_tail"], prm["a_ln_g"], prm["a_ln_b"], wm, bias,
                                            min(m, 512), chunk, want_vn)

    sh0 = _pad_rows(shift_prev.astype(F32), D_B)
    s0 = wkv0.astype(F32).reshape(bsz, N_QUAD, QUAD, HEAD_B)
    out_b, st_new = _rwkv(proj, tail, sh0, s0, prm["rwkv"], bsz, t, scan_prec)

    x_new, x_new_bf = _outproj_ln(out_a, 0, out_b.reshape(m, D_B), 0, x2, prm["w_out"], prm["ln_g"], prm["ln_b"],
                                  min(m, 512), True)
    last = proj.reshape(bsz, t, -1)[:, -1]
    shift_new = jnp.concatenate([last, tail.reshape(bsz, t, -1)[:, -1, D_B:]], axis=-1)
    v_n = maybe_vn[0].reshape(bsz, t, D_A) if want_vn else None
    return (x_new, x_new_bf), shift_new, st_new.reshape(bsz, H_B, HEAD_B, HEAD_B), v_n


def _c_layer(x2, xb, bsz, t, k_cache, v_cache, lam_init, w_in, lam_prm, subg, w_out, ln_g, ln_b):
    m = bsz * t
    tm = min(m, 512)
    q, = _cproj(xb, w_in, 0, "q", tm)
    k_leaf, kb = _cproj(xb, w_in, 1, "k", tm)
    v_leaf, vb = _cproj(xb, w_in, 2, "v", tm)
    g, = _cproj(xb, w_in, 3, "g", tm)
    if k_cache is None:
        o = _attn_prompt(q, kb, vb, g, lam_prm, subg, bsz, t, lam_init, 512)
    else:
        past = k_cache.shape[1]
        o = _attn_sample(q, kb, vb, g, k_cache.reshape(bsz * past, K_ROWS, HEAD_C),
                         v_cache.reshape(bsz * past, H_C, 2 * HEAD_C), lam_prm, subg, bsz, t, past, lam_init)
    x_new, = _outproj_ln(o, 0, o, 1, x2, w_out, ln_g, ln_b, min(m, 512), False)
    k_new = k_leaf.reshape(bsz, t, H_C, 2, HEAD_C)
    v_new = v_leaf.reshape(bsz, t, H_C, 2 * HEAD_C)
    return x_new.reshape(bsz, t, D_MODEL), k_new, v_new


def kernel(x_prompt, x_sample, state_b_shift, state_b_wkv, cache_c_k, cache_c_v, ab_w_in, ab_a_ln_g, ab_a_ln_b, ab_a_ws, ab_a_bs, ab_b_mu, ab_b_w0, ab_b_w2, ab_b_a0, ab_b_a2, ab_b_kk, ab_b_ka, ab_b_rk, ab_b_lnx_g, ab_b_lnx_b, ab_w_out, ab_ln_g, ab_ln_b, c_w_in, c_lam_q1, c_lam_k1, c_lam_q2, c_lam_k2, c_subln_g, c_w_out, c_ln_g, c_ln_b):
    scan_prec = None
    bp, tp = x_prompt.shape[:2]
    bs, ts = x_sample.shape[:2]
    x_p, x_s = x_prompt, x_sample
    xp_pair = xs_pair = None
    sh_p_l, wkv_p_l, sh_s_l, wkv_s_l, va_s_l = [], [], [], [], []
    kp_l, vp_l, ks_l, vs_l = [], [], [], []
    for li in range(DEPTH):
        j = li // 2
        if li % 2 == 0:
            prm = _ab_params(ab_w_in[j], ab_a_ln_g[j], ab_a_ln_b[j], ab_a_ws[j], ab_a_bs[j], ab_b_mu[j],
                             ab_b_w0[j], ab_b_w2[j], ab_b_a0[j], ab_b_a2[j], ab_b_kk[j], ab_b_ka[j],
                             ab_b_rk[j], ab_b_lnx_g[j], ab_b_lnx_b[j], ab_w_out[j], ab_ln_g[j], ab_ln_b[j])
            shift0 = jnp.zeros((bp, D_B_SHIFT), F32)
            wkv0 = jnp.zeros((bp, H_B, HEAD_B, HEAD_B), F32)
            xp_pair, sh_p, wkv_p, _ = _ab_layer(x_p, shift0, wkv0, prm, scan_prec, False)
            xs_pair, sh_s, wkv_s, va_s = _ab_layer(x_s, state_b_shift[j], state_b_wkv[j], prm, scan_prec, True)
            sh_p_l.append(sh_p)
            wkv_p_l.append(wkv_p)
            sh_s_l.append(sh_s)
            wkv_s_l.append(wkv_s)
            va_s_l.append(va_s)
        else:
            lam_init = 0.8 - 0.6 * math.exp(-0.3 * li)
            lam_prm = (_row(c_lam_q1[j]), _row(c_lam_k1[j]), _row(c_lam_q2[j]), _row(c_lam_k2[j]))
            args = (lam_init, c_w_in[j].astype(BF16), lam_prm, _row(c_subln_g[j]), c_w_out[j].astype(BF16), _row(c_ln_g[j]),
                    _row(c_ln_b[j]))
            x_p, k_p, v_p = _c_layer(*xp_pair, bp, tp, None, None, *args)
            x_s, k_s, v_s = _c_layer(*xs_pair, bs, ts, cache_c_k[j], cache_c_v[j], *args)
            kp_l.append(k_p)
            vp_l.append(v_p)
            ks_l.append(k_s)
            vs_l.append(v_s)
    return (x_p, x_s, jnp.stack(sh_p_l), jnp.stack(wkv_p_l), jnp.stack(sh_s_l), jnp.stack(wkv_s_l),
            jnp.stack(va_s_l), jnp.stack(kp_l), jnp.stack(vp_l), jnp.stack(ks_l), jnp.stack(vs_l))
```

```python
import functools
import math

import jax
import jax.numpy as jnp
from jax import lax
from jax.experimental import pallas as pl
from jax.experimental.pallas import tpu as pltpu

F32 = jnp.float32
BF16 = jnp.bfloat16

D_MODEL = 2048
DEPTH = 2
D_A = 1024
G_A = 4
C_A = D_A // G_A
MLP_CHUNK = 128
D_B = 1024
HEAD_B = 64
H_B = D_B // HEAD_B
R_W = 64
R_A = 64
D_B_SHIFT = 3 * D_B + R_W + R_A
HEAD_C = 128
H_C = 8
D_C = 2048
ATT_CHUNK = 64
DEEPNORM_ALPHA = (2 * DEPTH) ** 0.25
LN_EPS = 1e-5
GN_EPS_B = 64e-5
SUBLN_EPS = 1e-5

QUAD = 4 * HEAD_B
N_QUAD = D_B // QUAD
SCAN_L = 64
VMEM_LIMIT = 56 * 1024 * 1024


def _cparams(sem):
    return pltpu.CompilerParams(dimension_semantics=sem, vmem_limit_bytes=VMEM_LIMIT)


def _dot(a, b, prec=None):
    return jnp.dot(a, b, precision=prec, preferred_element_type=F32)


def _dot_nt(a, b, prec=None):
    return lax.dot_general(a, b, (((1,), (1,)), ((), ())), precision=prec, preferred_element_type=F32)


def _dot_tn(a, b, prec=None):
    return lax.dot_general(a, b, (((0,), (0,)), ((), ())), precision=prec, preferred_element_type=F32)


def _silu(x):
    return x * jax.nn.sigmoid(x)


N_A_COLS = 3
N_B_COLS = 3


def _abproj_kernel(x_ref, w_ref, wt_ref, lng_ref, lnb_ref, wm_ref, bias_ref, o_ref, ot_ref, oa_ref, *rest, chunk):
    *maybe_vn_ref, xb_ref, u_scr, v_scr, ga_scr, vn_scr = rest
    j = pl.program_id(1)

    def block():
        return _dot(xb_ref[...], w_ref[...])

    @pl.when(j == 0)
    def _():
        xb = x_ref[...].astype(BF16)
        xb_ref[...] = xb
        ot_ref[...] = _dot(xb, wt_ref[...])
        u_scr[...] = block()

    @pl.when(j == 1)
    def _():
        v_scr[...] = block()

    @pl.when(j == 2)
    def _():
        ga_scr[...] = block()
        v = v_scr[...]
        mu = jnp.mean(v, -1, keepdims=True)
        vc = v - mu
        var = jnp.mean(vc * vc, -1, keepdims=True)
        vn = vc * lax.rsqrt(var + LN_EPS) * lng_ref[...] + lnb_ref[...]
        for ref in maybe_vn_ref:
            ref[...] = vn
        vn_scr[...] = vn.astype(BF16)

    @pl.when(j == N_A_COLS)
    def _():
        o_ref[...] = block()
        for c in range(u_scr.shape[0] // chunk):
            rows = slice(c * chunk, (c + 1) * chunk)
            parts = [_dot(wm_ref[g], vn_scr[rows, g * C_A:(g + 1) * C_A]) for g in range(G_A)]
            sg = jnp.concatenate(parts, axis=-1) + bias_ref[...]
            oa_ref[rows, :] = (u_scr[rows, :] * sg * _silu(ga_scr[rows, :])).astype(oa_ref.dtype)

    @pl.when(j > N_A_COLS)
    def _():
        o_ref[...] = block()


def _ab_proj(x, w, w_tail, lng, lnb, wm, bias, tm, chunk, want_vn):
    m, k = x.shape
    n_col = N_A_COLS + N_B_COLS
    nt = w_tail.shape[1]
    full = lambda a: pl.BlockSpec(a.shape, lambda i, j: (0,) * a.ndim)
    row = lambda width: pl.BlockSpec((tm, width), lambda i, j: (i, 0))
    n_vn = 1 if want_vn else 0
    return pl.pallas_call(
        functools.partial(_abproj_kernel, chunk=chunk),
        grid=(m // tm, n_col),
        in_specs=[pl.BlockSpec((tm, k), lambda i, j: (i, 0)),
                  pl.BlockSpec((k, D_A), lambda i, j: (0, j)),
                  full(w_tail), full(lng), full(lnb), full(wm), full(bias)],
        out_specs=[pl.BlockSpec((tm, D_A), lambda i, j: (i, jnp.maximum(j - N_A_COLS, 0))),
                   row(nt), row(D_A)] + [row(D_A)] * n_vn,
        out_shape=[jax.ShapeDtypeStruct((m, (n_col - N_A_COLS) * D_A), F32), jax.ShapeDtypeStruct((m, nt), F32),
                   jax.ShapeDtypeStruct((m, D_A), BF16)] + [jax.ShapeDtypeStruct((m, D_A), F32)] * n_vn,
        scratch_shapes=[pltpu.VMEM((tm, k), BF16)] + [pltpu.VMEM((tm, D_A), F32)] * 3 + [pltpu.VMEM((tm, D_A), BF16)],
        name="ab_proj",
        compiler_params=_cparams(("parallel", "arbitrary")),
    )(x, w, w_tail, lng, lnb, wm, bias)


QK_SCALE = (HEAD_C ** -0.5) * math.log2(math.e)
K_ROWS = 2 * H_C


def _cproj_kernel(x_ref, w_ref, *o_refs, mode):
    acc = _dot(x_ref[...], w_ref[...])
    if mode == "q":
        o_refs[0][...] = (acc * QK_SCALE).astype(BF16)
    elif mode == "k":
        leaf_hbm, bf_ref, stage, sem = o_refs
        i = pl.program_id(0)
        n_steps = pl.num_programs(0)
        tm = acc.shape[0]
        slot = i % 2

        def leaf_copies(s, src):
            rows = pl.ds(s * tm, tm)
            return [pltpu.make_async_copy(stage.at[src, :, pl.ds(c * HEAD_C, HEAD_C)], leaf_hbm.at[rows, c],
                                          sem.at[src]) for c in range(K_ROWS)]

        bf_ref[...] = acc.astype(BF16)

        @pl.when(i >= 2)
        def _():
            for cp in leaf_copies(i - 2, slot):
                cp.wait()

        stage[slot] = acc
        for cp in leaf_copies(i, slot):
            cp.start()

        @pl.when(i == n_steps - 1)
        def _():
            @pl.when(i >= 1)
            def _():
                for cp in leaf_copies(i - 1, 1 - slot):
                    cp.wait()

            for cp in leaf_copies(i, slot):
                cp.wait()
    elif mode == "v":
        leaf_ref, bf_ref = o_refs
        leaf_ref[...] = acc
        bf_ref[...] = acc.astype(BF16)
    else:
        o_refs[0][...] = acc


def _cproj(xb, w, col, mode, tm):
    m, k = xb.shape
    n = D_C
    flat = lambda dt: (pl.BlockSpec((tm, n), lambda i: (i, 0)), jax.ShapeDtypeStruct((m, n), dt))
    if mode == "q":
        outs = [flat(BF16)]
    elif mode == "k":
        outs = [(pl.BlockSpec(memory_space=pl.ANY), jax.ShapeDtypeStruct((m, K_ROWS, HEAD_C), F32)), flat(BF16)]
    elif mode == "v":
        outs = [flat(F32), flat(BF16)]
    else:
        outs = [flat(F32)]
    scratch = [pltpu.VMEM((2, tm, n), F32), pltpu.SemaphoreType.DMA((2,))] if mode == "k" else []
    return pl.pallas_call(
        functools.partial(_cproj_kernel, mode=mode),
        grid=(m // tm,),
        in_specs=[pl.BlockSpec((tm, k), lambda i: (i, 0)), pl.BlockSpec((k, n), lambda i: (0, col))],
        out_specs=[o[0] for o in outs],
        out_shape=[o[1] for o in outs],
        scratch_shapes=scratch,
        name="cproj_" + mode,
        compiler_params=_cparams(("arbitrary" if mode == "k" else "parallel",)),
    )(xb, w)


def _rwkv_kernel(r_ref, k_ref, v_ref, g_ref, wa_ref, sh0_ref, s0_ref, mu_ref, w0_ref, w2_ref, a0_ref,
                 a2_ref, kkw_ref, kaw_ref, rkw_ref, lng_ref, lnb_ref, ob_ref, sout_ref, prev_scr, st_scr,
                 *, prec):
    L = SCAN_L
    nb = r_ref.shape[0]
    c = pl.program_id(1)

    row4 = lax.broadcasted_iota(jnp.int32, (QUAD, QUAD), 0)
    col4 = lax.broadcasted_iota(jnp.int32, (QUAD, QUAD), 1)
    same_head = (row4 // HEAD_B) == (col4 // HEAD_B)
    ones_head = same_head.astype(F32)

    @pl.when(c == 0)
    def _():
        prev_scr[...] = sh0_ref[...]
        spread = (lax.broadcasted_iota(jnp.int32, (HEAD_B, QUAD), 1) % HEAD_B
                  == lax.broadcasted_iota(jnp.int32, (HEAD_B, QUAD), 0)).astype(F32)
        for b in range(nb):
            for q in range(N_QUAD):
                st_scr[b, q] = _dot(s0_ref[b, q], spread, lax.Precision.HIGHEST) * ones_head
    ones_bf = same_head.astype(BF16)
    fast = prec is None
    mask_bd = ones_bf if fast else ones_head

    def cast(x):
        return x.astype(BF16) if fast else x

    def split2(x):
        hi = x.astype(BF16)
        return jnp.concatenate([hi, (x - hi.astype(F32)).astype(BF16)], axis=0)

    def head_sum(x):
        parts = []
        for q in range(N_QUAD):
            s = _dot(split2(x[:, q * QUAD:(q + 1) * QUAD]), ones_bf)
            parts.append(s[:L] + s[L:])
        return jnp.concatenate(parts, axis=-1)

    tl = lax.broadcasted_iota(jnp.int32, (L, L), 0)
    sl = lax.broadcasted_iota(jnp.int32, (L, L), 1)
    tri_incl = (sl <= tl).astype(F32)

    def prepare(b):
        def shifted(ref, idx, width):
            x = ref[b]
            first = lax.broadcasted_iota(jnp.int32, (L, width), 0) == 0
            prev = jnp.where(first, prev_scr[b, idx:idx + 1, :width], pltpu.roll(x, 1, 0))
            prev_scr[b, idx:idx + 1, :width] = x[L - 1:L, :]
            return x + (prev - x) * mu_ref[idx:idx + 1, :width]

        r = shifted(r_ref, 0, D_B)
        k = shifted(k_ref, 1, D_B)
        v = shifted(v_ref, 2, D_B)
        wa = shifted(wa_ref, 3, R_W + R_A)

        z = w0_ref[...] + _dot(jnp.tanh(wa).astype(BF16), w2_ref[...])
        log_decay = -math.exp(-0.5) * jax.nn.sigmoid(z)
        a = jax.nn.sigmoid(a0_ref[...] + _dot(wa.astype(BF16), a2_ref[...]))

        kk = k * kkw_ref[...]
        kk = kk * lax.rsqrt(jnp.maximum(head_sum(kk * kk), 1e-24))
        k_mod = k * (1.0 + (a - 1.0) * kaw_ref[...])
        bonus = head_sum(r * k_mod * rkw_ref[...]) * v

        cum = _dot(tri_incl, log_decay, lax.Precision.HIGHEST)
        cum_last = cum[L - 1:L, :]
        p_inv = jnp.exp(-cum)
        p_last = jnp.exp(cum_last)
        bet = kk * a * p_inv
        kt = k_mod * p_inv
        return dict(
            kap=cast(kk * jnp.exp(cum - log_decay)), bet=cast(bet), kt=cast(kt), rt=cast(r * jnp.exp(cum)),
            kt_tail=cast(kt * p_last), bet_tail=cast(-(bet * p_last)), vc=cast(v), p_last=p_last, bonus=bonus)

    tq = lax.broadcasted_iota(jnp.int32, (L, QUAD), 0)
    sq = lax.broadcasted_iota(jnp.int32, (L, QUAD), 1) % HEAD_B
    strict = sq < tq
    incl = sq <= tq
    eye_c = (sq == tq).astype(F32)
    blk16 = (tq // 16) == (sq // 16)
    blk32 = (tq // 32) == (sq // 32)
    strict16 = strict & blk16
    off32 = strict & blk32 & jnp.logical_not(blk16)
    off64 = strict & jnp.logical_not(blk32)

    def bd(x):
        return jnp.concatenate([x] * (QUAD // L), axis=0) * mask_bd

    def mm(x, y_bd):
        return _dot(cast(x), y_bd, prec)

    pre = [prepare(b) for b in range(nb)]

    chains = [(b, q) for b in range(nb) for q in range(N_QUAD)]
    n = range(len(chains))
    lanes = [slice(q * QUAD, (q + 1) * QUAD) for _, q in chains]
    sub = lambda name, i: pre[chains[i][0]][name][:, lanes[i]]
    st = [st_scr[b, q] for b, q in chains]
    kr = [jnp.concatenate([sub("kap", i), sub("rt", i)], axis=0) for i in n]
    g_b = [_dot_nt(kr[i], bd(sub("bet", i)), prec) for i in n]
    g_k = [_dot_nt(kr[i], bd(sub("kt", i)), prec) for i in n]
    g_s = [_dot_nt(kr[i], cast(st[i]), prec) for i in n]
    bd_v = [bd(sub("vc", i)) for i in n]

    pw = [jnp.where(strict16, -g_b[i][:L], 0.0) for i in n]
    t_inv = [eye_c + pw[i] for i in n]
    pw = [mm(pw[i], bd(cast(pw[i]))) for i in n]
    for _ in range(2):
        both = [mm(jnp.concatenate([t_inv[i], pw[i]], axis=0), bd(cast(pw[i]))) for i in n]
        t_inv = [t_inv[i] + both[i][:L] for i in n]
        pw = [both[i][L:] for i in n]
    t_inv = [t_inv[i] + mm(t_inv[i], bd(cast(pw[i]))) for i in n]
    rhs = [g_s[i][:L] + mm(jnp.where(strict, g_k[i][:L], 0.0), bd_v[i]) for i in n]
    for off_mask in (off32, off64):
        low = [mm(jnp.where(off_mask, g_b[i][:L], 0.0), bd(cast(t_inv[i]))) for i in n]
        t_inv = [t_inv[i] - mm(t_inv[i], bd(cast(low[i]))) for i in n]

    u_c = [cast(mm(t_inv[i], bd(cast(rhs[i])))) for i in n]
    ys = [g_s[i][L:] + mm(jnp.where(incl, g_k[i][L:], 0.0), bd_v[i])
          - mm(jnp.where(incl, g_b[i][L:], 0.0), bd(u_c[i])) for i in n]
    for i in n:
        b, q = chains[i]
        upd = _dot_tn(jnp.concatenate([sub("vc", i), u_c[i]], axis=0),
                      jnp.concatenate([sub("kt_tail", i), sub("bet_tail", i)], axis=0), prec)
        st_scr[b, q] = (st[i] * sub("p_last", i) + upd) * ones_head

    @pl.when(c == pl.num_programs(1) - 1)
    def _():
        gather = (lax.broadcasted_iota(jnp.int32, (QUAD, HEAD_B), 0) % HEAD_B
                  == lax.broadcasted_iota(jnp.int32, (QUAD, HEAD_B), 1)).astype(F32)
        for b in range(nb):
            for q in range(N_QUAD):
                sout_ref[b, q] = _dot(st_scr[b, q], gather, lax.Precision.HIGHEST)

    for b in range(nb):
        y = jnp.concatenate(ys[b * N_QUAD:(b + 1) * N_QUAD], axis=-1)
        mu_y = head_sum(y) * (1.0 / HEAD_B)
        yc = y - mu_y
        var_y = head_sum(yc * yc) * (1.0 / HEAD_B)
        yn = yc * lax.rsqrt(var_y + GN_EPS_B) * lng_ref[...] + lnb_ref[...]
        ob_ref[b] = ((yn + pre[b]["bonus"]) * _silu(g_ref[b])).astype(ob_ref.dtype)


SCAN_NB = 4


def _rwkv(proj, tail, sh0, s0, prm, bsz, t, prec):
    nb = SCAN_NB
    nc = t // SCAN_L
    proj3 = proj.reshape(bsz, t, -1)
    tail3 = tail.reshape(bsz, t, -1)
    col = lambda j: pl.BlockSpec((nb, SCAN_L, D_B), lambda b, c, j=j: (b, c, j))
    full = lambda a: pl.BlockSpec(a.shape, lambda b, c: (0,) * a.ndim)
    per_b = lambda a: pl.BlockSpec((nb,) + a.shape[1:], lambda b, c: (b,) + (0,) * (a.ndim - 1))
    return pl.pallas_call(
        functools.partial(_rwkv_kernel, prec=prec),
        grid=(bsz // nb, nc),
        in_specs=[col(0), col(1), col(2), col(0),
                  pl.BlockSpec((nb, SCAN_L, R_W + R_A), lambda b, c: (b, c, D_B // (R_W + R_A))),
                  per_b(sh0), per_b(s0)] + [full(p) for p in prm],
        out_specs=[pl.BlockSpec((nb, SCAN_L, D_B), lambda b, c: (b, c, 0)), per_b(s0)],
        out_shape=[jax.ShapeDtypeStruct((bsz, t, D_B), BF16), jax.ShapeDtypeStruct(s0.shape, F32)],
        scratch_shapes=[pltpu.VMEM((nb, 4, D_B), F32), pltpu.VMEM((nb, N_QUAD, QUAD, QUAD), F32)],
        name="rwkv_scan",
        compiler_params=_cparams(("parallel", "arbitrary")),
    )(proj3, proj3, proj3, tail3, tail3, sh0, s0, *prm)


def _outproj_ln_kernel(a_ref, b_ref, x_ref, w1_ref, w2_ref, g_ref, bias_ref, o_ref, *maybe_bf_ref):
    acc = _dot(a_ref[...], w1_ref[...]) + _dot(b_ref[...], w2_ref[...])
    y = DEEPNORM_ALPHA * x_ref[...] + acc
    mu = jnp.mean(y, -1, keepdims=True)
    yc = y - mu
    var = jnp.mean(yc * yc, -1, keepdims=True)
    out = yc * lax.rsqrt(var + LN_EPS) * g_ref[...] + bias_ref[...]
    o_ref[...] = out
    for ref in maybe_bf_ref:
        ref[...] = out.astype(BF16)


def _outproj_ln(a, a_col, b, b_col, x, w, g, bias, tm, want_bf16):
    m = x.shape[0]
    half = D_MODEL // 2
    full = lambda arr: pl.BlockSpec(arr.shape, lambda i: (0,) * arr.ndim)
    row = pl.BlockSpec((tm, D_MODEL), lambda i: (i, 0))
    n_out = 2 if want_bf16 else 1
    return pl.pallas_call(
        _outproj_ln_kernel,
        grid=(m // tm,),
        in_specs=[pl.BlockSpec((tm, half), lambda i: (i, a_col)),
                  pl.BlockSpec((tm, half), lambda i: (i, b_col)),
                  row, pl.BlockSpec((half, D_MODEL), lambda i: (0, 0)),
                  pl.BlockSpec((half, D_MODEL), lambda i: (1, 0)), full(g), full(bias)],
        out_specs=[row] * n_out,
        out_shape=[jax.ShapeDtypeStruct((m, D_MODEL), F32), jax.ShapeDtypeStruct((m, D_MODEL), BF16)][:n_out],
        name="outproj_ln",
        compiler_params=_cparams(("parallel",)),
    )(a, b, x, w, w, g, bias)


def _lambda(lq1_ref, lk1_ref, lq2_ref, lk2_ref, lam_init):
    s1 = jnp.sum(lq1_ref[...] * lk1_ref[...], -1, keepdims=True)
    s2 = jnp.sum(lq2_ref[...] * lk2_ref[...], -1, keepdims=True)
    return jnp.exp(s1) - jnp.exp(s2) + lam_init


def _attn_finish(o, g, subg_ref, lam_init):
    o = o * lax.rsqrt(jnp.mean(o * o, -1, keepdims=True) + SUBLN_EPS) * subg_ref[...]
    return o * (1.0 - lam_init) * _silu(g)


ATT_ROWS = 128
LANES = 128


def _attn_prompt_kernel(q_ref, k_ref, v_ref, g_ref, lq1_ref, lk1_ref, lq2_ref, lk2_ref, subg_ref, o_ref,
                        m0_ref, m1_ref, l0_ref, l1_ref, acc0_ref, acc1_ref, sa_ref, sb_ref, *, tq, lam_init):
    qi = pl.program_id(2)
    lam = _lambda(lq1_ref, lk1_ref, lq2_ref, lk2_ref, lam_init)
    ms = (m0_ref, m1_ref)
    ls = (l0_ref, l1_ref)
    accs = (acc0_ref, acc1_ref)
    n_sub = tq // ATT_ROWS
    n_rep = tq // LANES
    for j in range(2):
        ms[j][...] = jnp.full((tq, LANES), -jnp.inf, F32)
        ls[j][...] = jnp.zeros((tq, LANES), F32)
        accs[j][...] = jnp.zeros((tq, 2 * HEAD_C), F32)

    chains = [(r, j) for r in range(n_sub) for j in range(2)]

    def blk(i):
        return pl.multiple_of(i * tq, tq)

    def put_scores(r, j, off, s_ref):
        rows = pl.ds(r * ATT_ROWS, ATT_ROWS)
        cols = slice(j * HEAD_C, (j + 1) * HEAD_C)
        s_ref[j, rows, :] = _dot_nt(q_ref[rows, cols], k_ref[pl.ds(off, tq), cols])

    def consume(r, j, s_ref, off, masked):
        rows = pl.ds(r * ATT_ROWS, ATT_ROWS)
        nk = (r + 1) * ATT_ROWS if masked else tq
        s = s_ref[j, rows, :nk]
        if masked:
            rq = (lax.broadcasted_iota(jnp.int32, (ATT_ROWS, nk), 0) + r * ATT_ROWS) // ATT_CHUNK
            ck = lax.broadcasted_iota(jnp.int32, (ATT_ROWS, nk), 1) // ATT_CHUNK
            s = jnp.where(ck <= rq, s, -jnp.inf)
        m_old = ms[j][rows, :]
        m_new = jnp.maximum(m_old, jnp.max(s, -1, keepdims=True))
        alpha = jnp.exp2(m_old - m_new)
        p = jnp.exp2(s - jnp.concatenate([m_new] * (nk // LANES), axis=1))
        part = p[:, :LANES]
        for c in range(1, nk // LANES):
            part = part + p[:, c * LANES:(c + 1) * LANES]
        ls[j][rows, :] = alpha * ls[j][rows, :] + part
        accs[j][rows, :] = (jnp.concatenate([alpha] * 2, axis=1) * accs[j][rows, :]
                            + _dot(p.astype(BF16), v_ref[pl.ds(off, nk), :]))
        ms[j][rows, :] = m_new

    def step(s_cur, off_cur, masked, s_next=None, off_next=None):
        for r, j in chains:
            if s_next is not None:
                put_scores(r, j, off_next, s_next)
            consume(r, j, s_cur, off_cur, masked)

    for r, j in chains:
        put_scores(r, j, blk(0), sa_ref)

    def pair(p, carry):
        step(sa_ref, blk(2 * p), False, sb_ref, blk(2 * p + 1))
        step(sb_ref, blk(2 * p + 1), False, sa_ref, blk(2 * p + 2))
        return carry

    lax.fori_loop(0, qi // 2, pair, 0)

    @pl.when(qi % 2 == 1)
    def _():
        step(sa_ref, blk(qi - 1), False, sb_ref, blk(qi))
        step(sb_ref, blk(qi), True)

    @pl.when(qi % 2 == 0)
    def _():
        step(sa_ref, blk(qi), True)

    l0 = jnp.sum(l0_ref[...], -1, keepdims=True)
    l1 = jnp.sum(l1_ref[...], -1, keepdims=True)
    o = acc0_ref[...] / l0 - lam * (acc1_ref[...] / l1)
    o_ref[...] = _attn_finish(o, g_ref[...], subg_ref, lam_init).astype(o_ref.dtype)


def _attn_prompt(q, k, v, g, lam_prm, subg, bsz, t, lam_init, tq):
    nq = t // tq
    hw = 2 * HEAD_C
    full = lambda a: pl.BlockSpec(a.shape, lambda b, h, i: (0,) * a.ndim)
    tile = pl.BlockSpec((tq, hw), lambda b, h, i: (b * nq + i, h))
    whole = pl.BlockSpec((t, hw), lambda b, h, i: (b, h))
    return pl.pallas_call(
        functools.partial(_attn_prompt_kernel, tq=tq, lam_init=lam_init),
        grid=(bsz, H_C, nq),
        in_specs=[tile, whole, whole, tile] + [full(p) for p in lam_prm] + [full(subg)],
        out_specs=tile,
        out_shape=jax.ShapeDtypeStruct((bsz * t, D_C), BF16),
        scratch_shapes=([pltpu.VMEM((tq, LANES), F32)] * 4 + [pltpu.VMEM((tq, hw), F32)] * 2
                        + [pltpu.VMEM((2, tq, tq), F32)] * 2),
        name="attn_prompt",
        compiler_params=_cparams(("parallel", "parallel", "arbitrary")),
    )(q, k, v, g, *lam_prm, subg)


def _attn_sample_kernel(q_ref, kn_ref, vn_ref, g_ref, kc_hbm, vc_hbm, lq1_ref, lk1_ref, lq2_ref,
                        lk2_ref, subg_ref, o_ref, kbuf, vbuf, sem, *, lam_init, past):
    step = pl.program_id(0)
    slot = step % 2

    def cache_copies(s, into):
        b = s // H_C
        h = s % H_C
        rows = pl.ds(b * past, past)
        return (pltpu.make_async_copy(kc_hbm.at[rows, 2 * h], kbuf.at[into, 0], sem.at[into, 0]),
                pltpu.make_async_copy(kc_hbm.at[rows, 2 * h + 1], kbuf.at[into, 1], sem.at[into, 1]),
                pltpu.make_async_copy(vc_hbm.at[rows, h], vbuf.at[into], sem.at[into, 2]))

    @pl.when(step == 0)
    def _():
        for cp in cache_copies(0, 0):
            cp.start()

    @pl.when(step + 1 < pl.num_programs(0))
    def _():
        for cp in cache_copies(step + 1, 1 - slot):
            cp.start()

    for cp in cache_copies(step, slot):
        cp.wait()

    lam = _lambda(lq1_ref, lk1_ref, lq2_ref, lk2_ref, lam_init)
    q = q_ref[...]
    kn = kn_ref[...]
    probs = []
    for j in range(2):
        cols = slice(j * HEAD_C, (j + 1) * HEAD_C)
        sc = _dot_nt(q[:, cols], kbuf[slot, j].astype(BF16))
        sn = _dot_nt(q[:, cols], kn[:, cols])
        m = jnp.maximum(jnp.max(sc, -1, keepdims=True), jnp.max(sn, -1, keepdims=True))
        pc = jnp.exp2(sc - m)
        pn = jnp.exp2(sn - m)
        inv = 1.0 / (jnp.sum(pc, -1, keepdims=True) + jnp.sum(pn, -1, keepdims=True))
        probs.append((pc * inv, pn * inv))
    ac = probs[0][0] - lam * probs[1][0]
    an = probs[0][1] - lam * probs[1][1]
    o = _dot(ac.astype(BF16), vbuf[slot].astype(BF16)) + _dot(an.astype(BF16), vn_ref[...])
    o_ref[...] = _attn_finish(o, g_ref[...], subg_ref, lam_init).astype(o_ref.dtype)


def _attn_sample(q, kn, vn, g, kc, vc, lam_prm, subg, bsz, t, past, lam_init):
    hw = 2 * HEAD_C
    full = lambda a: pl.BlockSpec(a.shape, lambda s: (0,) * a.ndim)
    new = pl.BlockSpec((t, hw), lambda s: (s // H_C, s % H_C))
    hbm = pl.BlockSpec(memory_space=pl.ANY)
    return pl.pallas_call(
        functools.partial(_attn_sample_kernel, lam_init=lam_init, past=past),
        grid=(bsz * H_C,),
        in_specs=[new, new, new, new, hbm, hbm] + [full(p) for p in lam_prm] + [full(subg)],
        out_specs=new,
        out_shape=jax.ShapeDtypeStruct((bsz * t, D_C), BF16),
        scratch_shapes=[pltpu.VMEM((2, 2, past, HEAD_C), F32), pltpu.VMEM((2, past, hw), F32),
                        pltpu.SemaphoreType.DMA((2, 3))],
        name="attn_sample",
        compiler_params=_cparams(("arbitrary",)),
    )(q, kn, vn, g, kc, vc, *lam_prm, subg)


def _row(p):
    return p.reshape(1, -1).astype(F32)


def _pad_rows(x, width):
    lead = x.shape[:-1]
    main = x[..., :3 * width].reshape(lead + (3, width))
    rest = jnp.pad(x[..., 3 * width:], [(0, 0)] * len(lead) + [(0, 4 * width - x.shape[-1])])
    return jnp.concatenate([main, rest[..., None, :]], axis=-2)


def _ab_params(w_in, a_ln_g, a_ln_b, a_ws, a_bs, b_mu, b_w0, b_w2, b_a0, b_a2, b_kk, b_ka, b_rk, b_lnx_g,
               b_lnx_b, w_out, ln_g, ln_b):
    split = 3 * D_A + 3 * D_B
    w_main = w_in.astype(BF16)
    w_tail = jnp.concatenate([w_in[:, split + R_W + R_A:], w_in[:, split:split + R_W + R_A]], axis=1).astype(BF16)
    zeros = jnp.zeros((R_W, D_B), F32)
    rwkv = (_pad_rows(b_mu, D_B), _row(b_w0),
            jnp.concatenate([b_w2, zeros], 0).astype(BF16), _row(b_a0),
            jnp.concatenate([zeros, b_a2], 0).astype(BF16),
            _row(b_kk), _row(b_ka), _row(b_rk), _row(b_lnx_g), _row(b_lnx_b))
    return dict(w_main=w_main, w_tail=w_tail, a_ln_g=_row(a_ln_g), a_ln_b=_row(a_ln_b), a_ws=a_ws, a_bs=a_bs,
                rwkv=rwkv, w_out=w_out.astype(BF16), ln_g=_row(ln_g), ln_b=_row(ln_b))


def _ab_layer(x, shift_prev, wkv0, prm, scan_prec, want_vn):
    bsz, t, _ = x.shape
    m = bsz * t
    x2 = x.reshape(m, D_MODEL)
    chunk = min(t, MLP_CHUNK)
    mask = jnp.tril(jnp.ones((chunk, chunk), dtype=bool))
    wm = jnp.where(mask, prm["a_ws"][:, :chunk, :chunk], 0).astype(BF16)
    bias = jnp.repeat(jnp.swapaxes(prm["a_bs"][:, :chunk], 0, 1), C_A, axis=1).astype(F32)
    proj, tail, out_a, *maybe_vn = _ab_proj(x2, prm["w_main"], prm["w_tail"], prm["a_ln_g"], prm["a_ln_b"], wm, bias,
                                            min(m, 512), chunk, want_vn)

    sh0 = _pad_rows(shift_prev.astype(F32), D_B)
    s0 = wkv0.astype(F32).reshape(bsz, N_QUAD, QUAD, HEAD_B)
    out_b, st_new = _rwkv(proj, tail, sh0, s0, prm["rwkv"], bsz, t, scan_prec)

    x_new, x_new_bf = _outproj_ln(out_a, 0, out_b.reshape(m, D_B), 0, x2, prm["w_out"], prm["ln_g"], prm["ln_b"],
                                  min(m, 512), True)
    last = proj.reshape(bsz, t, -1)[:, -1]
    shift_new = jnp.concatenate([last, tail.reshape(bsz, t, -1)[:, -1, D_B:]], axis=-1)
    v_n = maybe_vn[0].reshape(bsz, t, D_A) if want_vn else None
    return (x_new, x_new_bf), shift_new, st_new.reshape(bsz, H_B, HEAD_B, HEAD_B), v_n


def _c_layer(x2, xb, bsz, t, k_cache, v_cache, lam_init, w_in, lam_prm, subg, w_out, ln_g, ln_b):
    m = bsz * t
    tm = min(m, 512)
    q, = _cproj(xb, w_in, 0, "q", tm)
    k_leaf, kb = _cproj(xb, w_in, 1, "k", tm)
    v_leaf, vb = _cproj(xb, w_in, 2, "v", tm)
    g, = _cproj(xb, w_in, 3, "g", tm)
    if k_cache is None:
        o = _attn_prompt(q, kb, vb, g, lam_prm, subg, bsz, t, lam_init, 512)
    else:
        past = k_cache.shape[1]
        o = _attn_sample(q, kb, vb, g, k_cache.reshape(bsz * past, K_ROWS, HEAD_C),
                         v_cache.reshape(bsz * past, H_C, 2 * HEAD_C), lam_prm, subg, bsz, t, past, lam_init)
    x_new, = _outproj_ln(o, 0, o, 1, x2, w_out, ln_g, ln_b, min(m, 512), False)
    k_new = k_leaf.reshape(bsz, t, H_C, 2, HEAD_C)
    v_new = v_leaf.reshape(bsz, t, H_C, 2 * HEAD_C)
    return x_new.reshape(bsz, t, D_MODEL), k_new, v_new


def kernel(x_prompt, x_sample, state_b_shift, state_b_wkv, cache_c_k, cache_c_v, ab_w_in, ab_a_ln_g, ab_a_ln_b, ab_a_ws, ab_a_bs, ab_b_mu, ab_b_w0, ab_b_w2, ab_b_a0, ab_b_a2, ab_b_kk, ab_b_ka, ab_b_rk, ab_b_lnx_g, ab_b_lnx_b, ab_w_out, ab_ln_g, ab_ln_b, c_w_in, c_lam_q1, c_lam_k1, c_lam_q2, c_lam_k2, c_subln_g, c_w_out, c_ln_g, c_ln_b):
    scan_prec = None
    bp, tp = x_prompt.shape[:2]
    bs, ts = x_sample.shape[:2]
    x_p, x_s = x_prompt, x_sample
    xp_pair = xs_pair = None
    sh_p_l, wkv_p_l, sh_s_l, wkv_s_l, va_s_l = [], [], [], [], []
    kp_l, vp_l, ks_l, vs_l = [], [], [], []
    for li in range(DEPTH):
        j = li // 2
        if li % 2 == 0:
            prm = _ab_params(ab_w_in[j], ab_a_ln_g[j], ab_a_ln_b[j], ab_a_ws[j], ab_a_bs[j], ab_b_mu[j],
                             ab_b_w0[j], ab_b_w2[j], ab_b_a0[j], ab_b_a2[j], ab_b_kk[j], ab_b_ka[j],
                             ab_b_rk[j], ab_b_lnx_g[j], ab_b_lnx_b[j], ab_w_out[j], ab_ln_g[j], ab_ln_b[j])
            shift0 = jnp.zeros((bp, D_B_SHIFT), F32)
            wkv0 = jnp.zeros((bp, H_B, HEAD_B, HEAD_B), F32)
            xp_pair, sh_p, wkv_p, _ = _ab_layer(x_p, shift0, wkv0, prm, scan_prec, False)
            xs_pair, sh_s, wkv_s, va_s = _ab_layer(x_s, state_b_shift[j], state_b_wkv[j], prm, scan_prec, True)
            sh_p_l.append(sh_p)
            wkv_p_l.append(wkv_p)
            sh_s_l.append(sh_s)
            wkv_s_l.append(wkv_s)
            va_s_l.append(va_s)
        else:
            lam_init = 0.8 - 0.6 * math.exp(-0.3 * li)
            lam_prm = (_row(c_lam_q1[j]), _row(c_lam_k1[j]), _row(c_lam_q2[j]), _row(c_lam_k2[j]))
            args = (lam_init, c_w_in[j].astype(BF16), lam_prm, _row(c_subln_g[j]), c_w_out[j].astype(BF16), _row(c_ln_g[j]),
                    _row(c_ln_b[j]))
            x_p, k_p, v_p = _c_layer(*xp_pair, bp, tp, None, None, *args)
            x_s, k_s, v_s = _c_layer(*xs_pair, bs, ts, cache_c_k[j], cache_c_v[j], *args)
            kp_l.append(k_p)
            vp_l.append(v_p)
            ks_l.append(k_s)
            vs_l.append(v_s)
    return (x_p, x_s, jnp.stack(sh_p_l), jnp.stack(wkv_p_l), jnp.stack(sh_s_l), jnp.stack(wkv_s_l),
            jnp.stack(va_s_l), jnp.stack(kp_l), jnp.stack(vp_l), jnp.stack(ks_l), jnp.stack(vs_l))
```
